```python
import math
import jax, jax.numpy as jnp
from jax import lax
import numpy as np

D_MODEL = 2048
BATCH = 4
SEQ = 4096
DEPTH = 4

HEAD_DIM = 128
N_MIX_HEADS = D_MODEL // HEAD_DIM
A_HEADS = N_MIX_HEADS // 4
B_HEADS = (N_MIX_HEADS - A_HEADS) // 2
C_HEADS = N_MIX_HEADS - A_HEADS - B_HEADS
A_DK = HEAD_DIM
A_DV = HEAD_DIM
A_W = A_HEADS * A_DK
B_W = B_HEADS * HEAD_DIM
C_W = C_HEADS * HEAD_DIM
MIX_W = A_W + B_W + C_W
IN_SPLITS = [A_W, A_W, A_W, A_W, B_W, B_W, B_W, B_HEADS, C_W, C_W, C_W]
IN_COLS = sum(IN_SPLITS)
A_CHUNK = 64
B_Q_BLOCK = 128
C_BLOCK = 256
C_TOPK = 3
C_Q_BLOCK = 64
D_FF = 5632
CONV_W = 3
PLE_DIM = 256
EPS = 1e-6
TINY = 1e-30
NEG = -1e30

kernel_name = "hymba_style_hgrn2_fox_moba_hybrid"


def rmsnorm(x, g):
    xf = x.astype(jnp.float32)
    y = xf * lax.rsqrt(jnp.mean(xf * xf, axis=-1, keepdims=True) + EPS)
    return (y * g.astype(jnp.float32)).astype(x.dtype)


def heads(t, n):
    b, s, _ = t.shape
    return t.reshape(b, s, n, -1).transpose(0, 2, 1, 3)


def merge_blocks(o):
    nqb, b, h, q, d = o.shape
    return o.transpose(1, 0, 3, 2, 4).reshape(b, nqb * q, h * d)


def alibi_slopes(n):
    return jnp.exp2(-8.0 * jnp.arange(1, n + 1, dtype=jnp.float32) / n)


def hgrn2_lower_bounds(lb_logits):
    sm = jax.nn.softmax(lb_logits.astype(jnp.float32), axis=0)
    return jnp.cumsum(sm, axis=0) - sm[0:1]


def hgrn2_mixer(q, f_logit, i, g, lb, o_norm_g):
    bsz, s, _ = q.shape
    dt = q.dtype
    z = f_logit.astype(jnp.float32)
    lbf = lb.astype(jnp.float32)
    f = lbf + (1.0 - lbf) * jax.nn.sigmoid(z)
    log_f = jnp.log(jnp.maximum(f, TINY))
    k = (1.0 - lbf) * jax.nn.sigmoid(-z)
    qh = heads(jax.nn.silu(q.astype(jnp.float32)), A_HEADS)
    kh = heads(k, A_HEADS)
    gh = heads(log_f, A_HEADS)
    vh = heads(i.astype(jnp.float32), A_HEADS)
    nc = s // A_CHUNK

    def chunks(t):
        return t.reshape(bsz, A_HEADS, nc, A_CHUNK, t.shape[-1]).transpose(2, 0, 1, 3, 4)

    causal = jnp.tril(jnp.ones((A_CHUNK, A_CHUNK), dtype=bool))[:, :, None]

    def step(state, xs):
        qc, kc, gc, vc = xs
        b = jnp.cumsum(gc, axis=2)
        o_inter = jnp.einsum('bhtd,bhde->bhte', qc * jnp.exp(b), state)
        diff = b[:, :, :, None, :] - b[:, :, None, :, :]
        decay = jnp.where(causal, jnp.exp(jnp.where(causal, diff, 0.0)), 0.0)
        attn = jnp.einsum('bhtd,bhsd,bhtsd->bhts', qc, kc, decay)
        o = o_inter + jnp.einsum('bhts,bhse->bhte', attn, vc)
        b_last = b[:, :, -1:, :]
        k_dec = kc * jnp.exp(b_last - b)
        state = jnp.exp(b_last[:, :, 0, :])[..., None] * state + jnp.einsum('bhsd,bhse->bhde', k_dec, vc)
        return state, o

    s0 = jnp.zeros((bsz, A_HEADS, A_DK, A_DV), jnp.float32)
    _, o = lax.scan(step, s0, (chunks(qh), chunks(kh), chunks(gh), chunks(vh)))
    o = o.transpose(1, 2, 0, 3, 4).reshape(bsz, A_HEADS, s, A_DV)
    o = rmsnorm(o, o_norm_g) * jax.nn.silu(heads(g.astype(jnp.float32), A_HEADS))
    return o.transpose(0, 2, 1, 3).reshape(bsz, s, A_W).astype(dt)


def fox_mixer(q, k, v, f_logit, b_f, qn_g, kn_g):
    bsz, s, _ = q.shape
    dt = q.dtype
    qh = rmsnorm(heads(q, B_HEADS), qn_g)
    kh = rmsnorm(heads(k, B_HEADS), kn_g)
    vh = heads(v, B_HEADS)
    log_f = jax.nn.log_sigmoid(f_logit.astype(jnp.float32) + b_f.astype(jnp.float32))
    fcum = jnp.cumsum(log_f, axis=1).transpose(0, 2, 1)
    nqb = s // B_Q_BLOCK
    qb = qh.reshape(bsz, B_HEADS, nqb, B_Q_BLOCK, HEAD_DIM).transpose(2, 0, 1, 3, 4)
    fq = fcum.reshape(bsz, B_HEADS, nqb, B_Q_BLOCK).transpose(2, 0, 1, 3)
    kpos = jnp.arange(s)
    scale = HEAD_DIM ** -0.5

    def block(xs):
        qi, fi, bi = xs
        qpos = bi * B_Q_BLOCK + jnp.arange(B_Q_BLOCK)
        logits = (jnp.einsum('bhqd,bhkd->bhqk', qi, kh).astype(jnp.float32) * scale
                  + fi[..., None] - fcum[:, :, None, :])
        logits = jnp.where(kpos[None, :] <= qpos[:, None], logits, NEG)
        w = jax.nn.softmax(logits, axis=-1).astype(dt)
        return jnp.einsum('bhqk,bhkd->bhqd', w, vh)

    o = lax.map(block, (qb, fq, jnp.arange(nqb)))
    return merge_blocks(o)


def moba_mixer(q, k, v, qn_g, kn_g, slopes):
    bsz, s, _ = q.shape
    dt = q.dtype
    h = C_HEADS
    qh = rmsnorm(heads(q, h), qn_g)
    kh = rmsnorm(heads(k, h), kn_g)
    vh = heads(v, h)
    nb = -(-s // C_BLOCK)
    s_pad = nb * C_BLOCK
    pad = ((0, 0), (0, 0), (0, s_pad - s), (0, 0))
    kp = jnp.pad(kh, pad)
    vp = jnp.pad(vh, pad)
    kblk = kp.reshape(bsz, h, nb, C_BLOCK, HEAD_DIM)
    vblk = vp.reshape(bsz, h, nb, C_BLOCK, HEAD_DIM)
    pos = jnp.arange(s)
    own = pos // C_BLOCK
    topk = min(C_TOPK, nb - 1)
    scale = HEAD_DIM ** -0.5
    kflat = kblk.reshape(bsz * h * nb, C_BLOCK, HEAD_DIM)
    vflat = vblk.reshape(bsz * h * nb, C_BLOCK, HEAD_DIM)
    bh_off = (jnp.arange(bsz * h, dtype=jnp.int32) * nb).reshape(bsz, h, 1, 1)
    blk_off = jnp.arange(C_BLOCK)
    sl = slopes[None, :, None, None]
    nqb = s // C_Q_BLOCK

    def to_blocks(t):
        t = t.reshape((bsz, h, nqb, C_Q_BLOCK) + t.shape[3:])
        return jnp.moveaxis(t, 2, 0)

    if topk > 0:
        kbar = jnp.mean(kblk.astype(jnp.float32), axis=3)
        gate = jnp.einsum('bhsd,bhnd->bhsn', qh.astype(jnp.float32), kbar)
        past = jnp.arange(nb)[None, :] < own[:, None]
        gate = jnp.where(past, gate, NEG)
        gval, gidx = lax.top_k(gate, topk)
        gvalid = gval > NEG / 2
        xs = (to_blocks(qh), jnp.arange(nqb), to_blocks(gidx), to_blocks(gvalid))
    else:
        xs = (to_blocks(qh), jnp.arange(nqb))

    def block(xs):
        qi, bi = xs[0], xs[1]
        qpos = bi * C_Q_BLOCK + jnp.arange(C_Q_BLOCK)
        start = (bi * C_Q_BLOCK // C_BLOCK) * C_BLOCK
        k_own = lax.dynamic_slice_in_dim(kp, start, C_BLOCK, axis=2)
        v_own = lax.dynamic_slice_in_dim(vp, start, C_BLOCK, axis=2)
        dist = (qpos[:, None] - (start + blk_off)[None, :]).astype(jnp.float32)
        s_own = jnp.einsum('bhqd,bhkd->bhqk', qi, k_own).astype(jnp.float32) * scale - sl * dist
        scores = [jnp.where(dist >= 0, s_own, NEG)]
        flats = []
        if topk > 0:
            idx, valid = xs[2], xs[3]
            flat = bh_off + idx
            for j in range(topk):
                fj = flat[..., j]
                flats.append(fj)
                kg = kflat[fj]
                kpos_g = idx[..., j][..., None] * C_BLOCK + blk_off
                dist_g = (qpos[None, None, :, None] - kpos_g).astype(jnp.float32)
                sj = jnp.einsum('bhqd,bhqkd->bhqk', qi, kg).astype(jnp.float32) * scale - sl * dist_g
                scores.append(jnp.where(valid[..., j][..., None], sj, NEG))
        w = jax.nn.softmax(jnp.concatenate(scores, axis=-1), axis=-1).astype(dt)
        out = jnp.einsum('bhqk,bhkd->bhqd', w[..., :C_BLOCK], v_own)
        for j, fj in enumerate(flats):
            vg = vflat[fj]
            wj = w[..., (j + 1) * C_BLOCK:(j + 2) * C_BLOCK]
            out = out + jnp.einsum('bhqk,bhqkd->bhqd', wj, vg)
        return out

    o = lax.map(block, xs)
    return merge_blocks(o)


def conv_ffn(x, w_gate, w_up, conv_w, conv_b, w_down):
    hg = x @ w_gate
    hg = lax.conv_general_dilated(hg, conv_w[:, None, :], window_strides=(1,),
                                  padding=[(CONV_W - 1, 0)],
                                  dimension_numbers=('NWC', 'WIO', 'NWC'),
                                  feature_group_count=hg.shape[-1]) + conv_b
    return (jax.nn.gelu(hg, approximate=True) * (x @ w_up)) @ w_down


def setup_inputs(seed: int = 0) -> dict:
    key = jax.random.key(seed)
    ks = jax.random.split(key, 21)
    f32 = jnp.float32

    def nrm(k, shape, scale):
        return jax.random.normal(k, shape, f32) * scale

    return {
        "x": nrm(ks[0], (BATCH, SEQ, D_MODEL), 1.0),
        "p": nrm(ks[1], (DEPTH, BATCH, SEQ, PLE_DIM), 1.0),
        "attn_norm": 1.0 + nrm(ks[2], (DEPTH, D_MODEL), 0.02),
        "w_in": nrm(ks[3], (DEPTH, D_MODEL, IN_COLS), D_MODEL ** -0.5),
        "fox_bf": nrm(ks[4], (DEPTH, B_HEADS), 0.1),
        "lb_logits": nrm(ks[5], (DEPTH, A_W), 0.1),
        "hgrn_onorm": 1.0 + nrm(ks[6], (DEPTH, A_DV), 0.02),
        "fox_qnorm": 1.0 + nrm(ks[7], (DEPTH, HEAD_DIM), 0.02),
        "fox_knorm": 1.0 + nrm(ks[8], (DEPTH, HEAD_DIM), 0.02),
        "moba_qnorm": 1.0 + nrm(ks[9], (DEPTH, HEAD_DIM), 0.02),
        "moba_knorm": 1.0 + nrm(ks[10], (DEPTH, HEAD_DIM), 0.02),
        "w_out": nrm(ks[11], (DEPTH, MIX_W, D_MODEL), MIX_W ** -0.5),
        "ffn_norm": 1.0 + nrm(ks[12], (DEPTH, D_MODEL), 0.02),
        "w_gate": nrm(ks[13], (DEPTH, D_MODEL, D_FF), D_MODEL ** -0.5),
        "w_up": nrm(ks[14], (DEPTH, D_MODEL, D_FF), D_MODEL ** -0.5),
        "conv_w": nrm(ks[15], (DEPTH, CONV_W, D_FF), CONV_W ** -0.5),
        "conv_b": nrm(ks[16], (DEPTH, D_FF), 0.01),
        "w_down": nrm(ks[17], (DEPTH, D_FF, D_MODEL), D_FF ** -0.5),
        "ple_norm": 1.0 + nrm(ks[18], (DEPTH, D_MODEL), 0.02),
        "w_ple_gate": nrm(ks[19], (DEPTH, D_MODEL, D_MODEL), D_MODEL ** -0.5),
        "w_ple_proj": nrm(ks[20], (DEPTH, PLE_DIM, D_MODEL), PLE_DIM ** -0.5),
    }


def reference(x, p, attn_norm, w_in, fox_bf, lb_logits, hgrn_onorm, fox_qnorm, fox_knorm,
              moba_qnorm, moba_knorm, w_out, ffn_norm, w_gate, w_up, conv_w, conv_b, w_down,
              ple_norm, w_ple_gate, w_ple_proj):
    lbs = hgrn2_lower_bounds(lb_logits)
    slopes = alibi_slopes(C_HEADS)
    split_idx = []
    acc = 0
    for n in IN_SPLITS[:-1]:
        acc += n
        split_idx.append(acc)
    h = x
    for i in range(DEPTH):
        a = rmsnorm(h, attn_norm[i])
        proj = a @ w_in[i]
        aq, af, ai, ag, bq, bk, bv, bf, cq, ck, cv = jnp.split(proj, split_idx, axis=-1)
        oa = hgrn2_mixer(aq, af, ai, ag, lbs[i], hgrn_onorm[i])
        ob = fox_mixer(bq, bk, bv, bf, fox_bf[i], fox_qnorm[i], fox_knorm[i])
        oc = moba_mixer(cq, ck, cv, moba_qnorm[i], moba_knorm[i], slopes)
        h = h + jnp.concatenate([oa, ob, oc], axis=-1) @ w_out[i]
        c = rmsnorm(h, ffn_norm[i])
        h = h + conv_ffn(c, w_gate[i], w_up[i], conv_w[i], conv_b[i], w_down[i])
        e = rmsnorm(h, ple_norm[i])
        h = h + jax.nn.sigmoid(e @ w_ple_gate[i]) * (p[i] @ w_ple_proj[i])
    return h
```

```python
import functools

import numpy as np
import jax
import jax.numpy as jnp
from jax import lax
from jax.experimental import pallas as pl
from jax.experimental.pallas import tpu as pltpu

F32 = jnp.float32
BF16 = jnp.bfloat16
HIGHEST = lax.Precision.HIGHEST

HEAD_DIM = 128
A_HEADS = 4
B_HEADS = 6
C_HEADS = 6
A_W = A_HEADS * HEAD_DIM
B_W = B_HEADS * HEAD_DIM
C_W = C_HEADS * HEAD_DIM
MOBA_BLOCK = 256
MOBA_TOPK = 3
EPS = 1e-6
TINY = 1e-30
NEG = -1e30

CB_AQ, CB_AF, CB_AI, CB_AG = 0, 4, 8, 12
CB_BQ, CB_BK, CB_BV, CB_BF = 16, 22, 28, 34
CB_CQ, CB_CK, CB_CV = 35, 41, 47
IN_COLS_PADDED = 54 * HEAD_DIM

HGRN_CHUNK = 64
VMEM_LIMIT = 56 * 1024 * 1024

NT_DIMS = (((1,), (1,)), ((), ()))


def _params(*sem):
    return pltpu.CompilerParams(dimension_semantics=sem, vmem_limit_bytes=VMEM_LIMIT)


def _sigmoid(x):
    return 1.0 / (1.0 + jnp.exp(-x))


def _rms(x, g):
    ms = jnp.mean(x * x, axis=-1, keepdims=True)
    return (x * lax.rsqrt(ms + EPS)) * g


def _lower_bounds_kernel(x_ref, o_ref):
    x = x_ref[...]
    depth = x.shape[0]
    m = x[0:1]
    for i in range(1, depth):
        m = jnp.maximum(m, x[i:i + 1])
    e = jnp.exp(x - m)
    tot = e[0:1]
    for i in range(1, depth):
        tot = tot + e[i:i + 1]
    sm = e / tot
    run = jnp.zeros_like(m)
    for i in range(depth):
        run = run + sm[i:i + 1]
        o_ref[i:i + 1, :] = run - sm[0:1]


def _lower_bounds(lb_logits):
    return pl.pallas_call(
        _lower_bounds_kernel,
        out_shape=jax.ShapeDtypeStruct(lb_logits.shape, F32),
    )(lb_logits.astype(F32))


def _norm_matmul_kernel(x_ref, g_ref, w_ref, o_ref, a_scr):
    @pl.when(pl.program_id(1) == 0)
    def _():
        a_scr[...] = _rms(x_ref[...], g_ref[...]).astype(BF16)

    o_ref[...] = jnp.dot(a_scr[...], w_ref[...], preferred_element_type=F32).astype(o_ref.dtype)


def _norm_matmul(x, g, w, *, tm, tn):
    t, k = x.shape
    n = w.shape[1]
    tm = min(tm, t)
    return pl.pallas_call(
        _norm_matmul_kernel,
        grid=(t // tm, n // tn),
        in_specs=[
            pl.BlockSpec((tm, k), lambda i, j: (i, 0)),
            pl.BlockSpec((1, k), lambda i, j: (0, 0)),
            pl.BlockSpec((k, tn), lambda i, j: (0, j)),
        ],
        out_specs=pl.BlockSpec((tm, tn), lambda i, j: (i, j)),
        out_shape=jax.ShapeDtypeStruct((t, n), F32),
        scratch_shapes=[pltpu.VMEM((tm, k), BF16)],
        compiler_params=_params("parallel", "arbitrary"),
        name="norm_in_proj",
    )(x, g.reshape(1, k), w)


def _hgrn_levels(chunk):
    levels = []
    m = chunk // 2
    while m >= 1:
        levels.append(m)
        m //= 2
    return levels


def _hgrn_masks(chunk):
    t = np.arange(chunk)[:, None]
    s = np.arange(chunk)[None, :]
    out = []
    for m in _hgrn_levels(chunk):
        same_parent = (t // (2 * m)) == (s // (2 * m))
        out.append((same_parent & ((t // m) % 2 == 1) & ((s // m) % 2 == 0)).astype(np.float32))
    return np.stack(out)


def _hgrn_kernel(q_ref, f_ref, i_ref, g_ref, lb_ref, on_ref, tri_ref, mask_ref, o_ref, st_scr, b_scr,
                 *, chunk, n_chunks):
    @pl.when(pl.program_id(2) == 0)
    def _():
        st_scr[...] = jnp.zeros_like(st_scr)

    lb = lb_ref[...]
    one_m_lb = 1.0 - lb
    onorm = on_ref[...]
    tri = tri_ref[...]
    row = lax.broadcasted_iota(jnp.int32, (chunk, HEAD_DIM), 0)

    def boundary_rows(m):
        if 2 * m >= 8:
            pieces = [
                jnp.broadcast_to(b_scr[2 * m * p + m - 1:2 * m * p + m, :], (2 * m, HEAD_DIM))
                for p in range(chunk // (2 * m))
            ]
            return pieces[0] if len(pieces) == 1 else jnp.concatenate(pieces, axis=0)
        assert m == 2
        lo = jnp.concatenate(
            [jnp.broadcast_to(b_scr[8 * p + 1:8 * p + 2, :], (8, HEAD_DIM)) for p in range(chunk // 8)], axis=0)
        hi = jnp.concatenate(
            [jnp.broadcast_to(b_scr[8 * p + 5:8 * p + 6, :], (8, HEAD_DIM)) for p in range(chunk // 8)], axis=0)
        return jnp.where((row & 4) == 0, lo, hi)

    def one_chunk(c, carry):
        rows = pl.ds(pl.multiple_of(c * chunk, chunk), chunk)
        z = f_ref[rows, :]
        f = lb + one_m_lb * _sigmoid(z)
        logf = jnp.log(jnp.maximum(f, TINY))
        k = one_m_lb * _sigmoid(-z)
        qraw = q_ref[rows, :]
        q = qraw * _sigmoid(qraw)
        v = i_ref[rows, :]
        v16 = v.astype(BF16)

        b = jnp.dot(tri, logf, precision=HIGHEST, preferred_element_type=F32)
        b_scr[...] = b

        attn = jnp.zeros((chunk, chunk), F32)
        for lev, m in enumerate(_hgrn_levels(chunk)):
            right = (row & m) != 0
            if m == 1:
                arg = jnp.where(right, logf, 0.0)
            else:
                d = b - boundary_rows(m)
                arg = jnp.where(right, d, -d)
            e = jnp.exp(arg)
            qt = jnp.where(right, q * e, 0.0).astype(BF16)
            kt = jnp.where(right, 0.0, k * e).astype(BF16)
            attn = attn + mask_ref[lev] * lax.dot_general(qt, kt, NT_DIMS, preferred_element_type=F32)
        diag = jnp.sum(q * k, axis=-1, keepdims=True)
        o = jnp.dot(attn.astype(BF16), v16, preferred_element_type=F32) + diag * v

        st = st_scr[...]
        qe = (q * jnp.exp(b)).astype(BF16)
        o = o + lax.dot_general(qe, st.astype(BF16), NT_DIMS, preferred_element_type=F32)

        b_last = b_scr[chunk - 1:chunk, :]
        k_dec = (k * jnp.exp(b_last - b)).astype(BF16)
        upd = jnp.dot(v.T.astype(BF16), k_dec, preferred_element_type=F32)
        st_scr[...] = st * jnp.exp(b_last) + upd

        gate = g_ref[rows, :]
        o_ref[rows, :] = (_rms(o, onorm) * (gate * _sigmoid(gate))).astype(o_ref.dtype)
        return carry

    lax.fori_loop(0, n_chunks, one_chunk, 0)


def _hgrn(proj3, lb, onorm, *, ts):
    bsz, s, _ = proj3.shape
    ts = min(ts, s)
    chunk = HGRN_CHUNK
    tri = jnp.asarray(np.tril(np.ones((chunk, chunk), np.float32)))
    masks = jnp.asarray(_hgrn_masks(chunk))
    nlev = masks.shape[0]

    def col(cb):
        return pl.BlockSpec((None, ts, HEAD_DIM), lambda b, h, i: (b, i, cb + h))

    return pl.pallas_call(
        functools.partial(_hgrn_kernel, chunk=chunk, n_chunks=ts // chunk),
        grid=(bsz, A_HEADS, s // ts),
        in_specs=[
            col(CB_AQ), col(CB_AF), col(CB_AI), col(CB_AG),
            pl.BlockSpec((1, HEAD_DIM), lambda b, h, i: (0, h)),
            pl.BlockSpec((1, HEAD_DIM), lambda b, h, i: (0, 0)),
            pl.BlockSpec((chunk, chunk), lambda b, h, i: (0, 0)),
            pl.BlockSpec((nlev, chunk, chunk), lambda b, h, i: (0, 0, 0)),
        ],
        out_specs=pl.BlockSpec((None, ts, HEAD_DIM), lambda b, h, i: (b, i, h)),
        out_shape=jax.ShapeDtypeStruct((bsz, s, A_W), BF16),
        scratch_shapes=[pltpu.VMEM((HEAD_DIM, HEAD_DIM), F32), pltpu.VMEM((chunk, HEAD_DIM), F32)],
        compiler_params=_params("parallel", "parallel", "arbitrary"),
        name="hgrn2",
    )(proj3, proj3, proj3, proj3, lb.reshape(1, A_W), onorm.reshape(1, HEAD_DIM), tri, masks)


def _qkv_prep_kernel(q_ref, k_ref, v_ref, qg_ref, kg_ref, qo_ref, ko_ref, vo_ref):
    qo_ref[...] = (_rms(q_ref[...], qg_ref[...]) * (HEAD_DIM ** -0.5)).astype(BF16)
    ko_ref[...] = _rms(k_ref[...], kg_ref[...]).astype(BF16)
    vo_ref[...] = v_ref[...].astype(BF16)


def _qkv_prep(proj3, qg, kg, cbq, cbk, cbv, n_heads, *, ts):
    bsz, s, _ = proj3.shape
    ts = min(ts, s)

    def col(cb):
        return pl.BlockSpec((None, ts, HEAD_DIM), lambda b, h, i: (b, i, cb + h))

    gain = pl.BlockSpec((1, HEAD_DIM), lambda b, h, i: (0, 0))
    out = pl.BlockSpec((None, None, ts, HEAD_DIM), lambda b, h, i: (b, h, i, 0))
    shp = jax.ShapeDtypeStruct((bsz, n_heads, s, HEAD_DIM), BF16)
    return pl.pallas_call(
        _qkv_prep_kernel,
        grid=(bsz, n_heads, s // ts),
        in_specs=[col(cbq), col(cbk), col(cbv), gain, gain],
        out_specs=[out, out, out],
        out_shape=[shp, shp, shp],
        compiler_params=_params("parallel", "parallel", "parallel"),
        name="qkv_prep",
    )(proj3, proj3, proj3, qg.reshape(1, HEAD_DIM), kg.reshape(1, HEAD_DIM))


def _fox_fcum_kernel(f_ref, bf_ref, tri_ref, o_ref, carry_scr):
    @pl.when(pl.program_id(1) == 0)
    def _():
        carry_scr[...] = jnp.zeros_like(carry_scr)

    x = f_ref[...] + bf_ref[...]
    logf = jnp.minimum(x, 0.0) - jnp.log(1.0 + jnp.exp(-jnp.abs(x)))
    cum = jnp.dot(tri_ref[...], logf, precision=HIGHEST, preferred_element_type=F32) + carry_scr[...]
    o_ref[...] = cum
    carry_scr[...] = cum[cum.shape[0] - 1:, :]


def _fox_fcum(proj3, b_f, *, ts):
    bsz, s, _ = proj3.shape
    ts = min(ts, s)
    tri = jnp.asarray(np.tril(np.ones((ts, ts), np.float32)))
    bf_pad = jnp.zeros((1, HEAD_DIM), F32).at[0, :B_HEADS].set(b_f.astype(F32))
    return pl.pallas_call(
        _fox_fcum_kernel,
        grid=(bsz, s // ts),
        in_specs=[
            pl.BlockSpec((None, ts, HEAD_DIM), lambda b, i: (b, i, CB_BF)),
            pl.BlockSpec((1, HEAD_DIM), lambda b, i: (0, 0)),
            pl.BlockSpec((ts, ts), lambda b, i: (0, 0)),
        ],
        out_specs=pl.BlockSpec((None, ts, HEAD_DIM), lambda b, i: (b, i, 0)),
        out_shape=jax.ShapeDtypeStruct((bsz, s, HEAD_DIM), F32),
        scratch_shapes=[pltpu.VMEM((1, HEAD_DIM), F32)],
        compiler_params=_params("parallel", "arbitrary"),
        name="fox_fcum",
    )(proj3, bf_pad, tri)


def _flash_update(carry, s, vb):
    m, l, acc = carry
    m_new = jnp.maximum(m, jnp.max(s, axis=-1, keepdims=True))
    p = jnp.exp(s - m_new)
    alpha = jnp.exp(m - m_new)
    l = alpha * l + jnp.sum(p, axis=-1, keepdims=True)
    acc = alpha * acc + jnp.dot(p.astype(BF16), vb, preferred_element_type=F32)
    return m_new, l, acc


def _flash_init(tq):
    return (jnp.full((tq, 1), -jnp.inf, F32), jnp.zeros((tq, 1), F32), jnp.zeros((tq, HEAD_DIM), F32))


def _fox_attn_kernel(q_ref, k_ref, v_ref, fk_ref, o_ref, *, tq, tk):
    qi = pl.program_id(2)
    q = q_ref[...]
    r = tq // tk

    def block(kj, carry, masked):
        rows = pl.ds(pl.multiple_of(kj * tk, tk), tk)
        s = lax.dot_general(q, k_ref[rows, :], NT_DIMS, preferred_element_type=F32) - fk_ref[kj]
        if masked:
            qpos = qi * tq + lax.broadcasted_iota(jnp.int32, (tq, tk), 0)
            kpos = kj * tk + lax.broadcasted_iota(jnp.int32, (tq, tk), 1)
            s = jnp.where(kpos <= qpos, s, NEG)
        return _flash_update(carry, s, v_ref[rows, :])

    carry = lax.fori_loop(0, qi * r, lambda kj, c: block(kj, c, False), _flash_init(tq))
    for d in range(r):
        carry = block(qi * r + d, carry, True)
    _, l, acc = carry
    o_ref[...] = (acc / l).astype(o_ref.dtype)


def _fox_attn(qn, kn, vb, fk, *, tq, tk):
    bsz, nh, s, _ = qn.shape
    tq = min(tq, s)
    tk = min(tk, tq)
    fk5 = fk.reshape(bsz, nh, s // tk, 1, tk)
    full = pl.BlockSpec((None, None, s, HEAD_DIM), lambda b, h, i: (b, h, 0, 0))
    return pl.pallas_call(
        functools.partial(_fox_attn_kernel, tq=tq, tk=tk),
        grid=(bsz, nh, s // tq),
        in_specs=[
            pl.BlockSpec((None, None, tq, HEAD_DIM), lambda b, h, i: (b, h, i, 0)),
            full, full,
            pl.BlockSpec((None, None, s // tk, 1, tk), lambda b, h, i: (b, h, 0, 0, 0)),
        ],
        out_specs=pl.BlockSpec((None, tq, HEAD_DIM), lambda b, h, i: (b, i, h)),
        out_shape=jax.ShapeDtypeStruct((bsz, s, nh * HEAD_DIM), BF16),
        compiler_params=_params("parallel", "parallel", "arbitrary"),
        name="fox_attn",
    )(qn, kn, vb, fk5)


def _moba_prep_kernel(q_ref, k_ref, v_ref, qg_ref, kg_ref, qo_ref, ko_ref, vo_ref, sel_ref, *, topk):
    qn = _rms(q_ref[...], qg_ref[...])
    kn = _rms(k_ref[...], kg_ref[...])
    qo_ref[...] = (qn * (HEAD_DIM ** -0.5)).astype(BF16)
    ko_ref[...] = kn.astype(BF16)
    vo_ref[...] = v_ref[...].astype(BF16)

    s = qn.shape[0]
    nb = s // MOBA_BLOCK
    kbar = jnp.sum(kn.reshape(nb, MOBA_BLOCK, HEAD_DIM), axis=1) * (1.0 / MOBA_BLOCK)
    kbar = jnp.concatenate([kbar, jnp.zeros((HEAD_DIM - nb, HEAD_DIM), F32)], axis=0)
    gate = lax.dot_general(qn, kbar, NT_DIMS, precision=HIGHEST, preferred_element_type=F32)
    lane = lax.broadcasted_iota(jnp.int32, (s, HEAD_DIM), 1)
    own = lax.broadcasted_iota(jnp.int32, (s, HEAD_DIM), 0) // MOBA_BLOCK
    gate = jnp.where(lane < own, gate, NEG)
    sel = jnp.zeros((s, HEAD_DIM), F32)
    for _ in range(topk):
        mx = jnp.max(gate, axis=-1, keepdims=True)
        idx = jnp.min(jnp.where(gate == mx, lane, HEAD_DIM), axis=-1, keepdims=True)
        hit = lane == idx
        sel = jnp.where(hit & (mx > NEG / 2), 1.0, sel)
        gate = jnp.where(hit, -jnp.inf, gate)
    sel_ref[...] = sel


def _moba_prep(proj3, qg, kg):
    bsz, s, _ = proj3.shape
    nb = s // MOBA_BLOCK
    topk = min(MOBA_TOPK, nb - 1)

    def col(cb):
        return pl.BlockSpec((None, s, HEAD_DIM), lambda b, h: (b, 0, cb + h))

    gain = pl.BlockSpec((1, HEAD_DIM), lambda b, h: (0, 0))
    out = pl.BlockSpec((None, None, s, HEAD_DIM), lambda b, h: (b, h, 0, 0))
    shp = jax.ShapeDtypeStruct((bsz, C_HEADS, s, HEAD_DIM), BF16)
    return pl.pallas_call(
        functools.partial(_moba_prep_kernel, topk=topk),
        grid=(bsz, C_HEADS),
        in_specs=[col(CB_CQ), col(CB_CK), col(CB_CV), gain, gain],
        out_specs=[out, out, out, out],
        out_shape=[shp, shp, shp, jax.ShapeDtypeStruct((bsz, C_HEADS, s, HEAD_DIM), F32)],
        compiler_params=_params("parallel", "parallel"),
        name="moba_prep",
    )(proj3, proj3, proj3, qg.reshape(1, HEAD_DIM), kg.reshape(1, HEAD_DIM))


def _moba_attn_kernel(q_ref, k_ref, v_ref, sel_ref, slope_ref, o_ref):
    tq = MOBA_BLOCK
    qi = pl.program_id(2)
    q = q_ref[...]
    sel = sel_ref[...]
    slope = slope_ref[0:1, 0:1]
    lane = lax.broadcasted_iota(jnp.int32, (tq, HEAD_DIM), 1)
    rel = lax.broadcasted_iota(jnp.int32, (tq, tq), 1) - lax.broadcasted_iota(jnp.int32, (tq, tq), 0)
    alibi = slope * rel.astype(F32)

    def scores(kj):
        rows = pl.ds(pl.multiple_of(kj * tq, tq), tq)
        s = lax.dot_general(q, k_ref[rows, :], NT_DIMS, preferred_element_type=F32)
        shift = slope * ((kj - qi) * tq).astype(F32)
        return s + alibi + shift, v_ref[rows, :]

    def past(kj, carry):
        s, vb = scores(kj)
        chosen = jnp.sum(jnp.where(lane == kj, sel, 0.0), axis=-1, keepdims=True)
        return _flash_update(carry, jnp.where(chosen > 0.5, s, NEG), vb)

    carry = lax.fori_loop(0, qi, past, _flash_init(tq))
    s, vb = scores(qi)
    _, l, acc = _flash_update(carry, jnp.where(rel <= 0, s, NEG), vb)
    o_ref[...] = (acc / l).astype(o_ref.dtype)


def _moba_attn(qn, kn, vb, sel):
    bsz, nh, s, _ = qn.shape
    tq = MOBA_BLOCK
    slopes = np.exp2(-8.0 * np.arange(1, nh + 1, dtype=np.float32) / nh).astype(np.float32)
    slopes = jnp.asarray(np.broadcast_to(slopes[:, None, None], (nh, 8, HEAD_DIM)).copy())
    full = pl.BlockSpec((None, None, s, HEAD_DIM), lambda b, h, i: (b, h, 0, 0))
    tile = pl.BlockSpec((None, None, tq, HEAD_DIM), lambda b, h, i: (b, h, i, 0))
    return pl.pallas_call(
        _moba_attn_kernel,
        grid=(bsz, nh, s // tq),
        in_specs=[tile, full, full, tile, pl.BlockSpec((None, 8, HEAD_DIM), lambda b, h, i: (h, 0, 0))],
        out_specs=pl.BlockSpec((None, tq, HEAD_DIM), lambda b, h, i: (b, i, h)),
        out_shape=jax.ShapeDtypeStruct((bsz, s, nh * HEAD_DIM), BF16),
        compiler_params=_params("parallel", "parallel", "arbitrary"),
        name="moba_attn",
    )(qn, kn, vb, sel, slopes)


def _out_proj_kernel(h_ref, a_ref, b_ref, c_ref, wa_ref, wb_ref, wc_ref, o_ref):
    acc = jnp.dot(a_ref[...], wa_ref[...], preferred_element_type=F32)
    acc = acc + jnp.dot(b_ref[...], wb_ref[...], preferred_element_type=F32)
    acc = acc + jnp.dot(c_ref[...], wc_ref[...], preferred_element_type=F32)
    o_ref[...] = h_ref[...] + acc


def _out_proj(h, oa, ob, oc, w_out, *, tm, tn):
    t, d = h.shape
    tm = min(tm, t)
    wa, wb, wc = w_out[:A_W], w_out[A_W:A_W + B_W], w_out[A_W + B_W:]

    def act(w):
        return pl.BlockSpec((tm, w), lambda i, j: (i, 0))

    def wgt(w):
        return pl.BlockSpec((w, tn), lambda i, j: (0, j))

    return pl.pallas_call(
        _out_proj_kernel,
        grid=(t // tm, d // tn),
        in_specs=[pl.BlockSpec((tm, tn), lambda i, j: (i, j)), act(A_W), act(B_W), act(C_W),
                  wgt(A_W), wgt(B_W), wgt(C_W)],
        out_specs=pl.BlockSpec((tm, tn), lambda i, j: (i, j)),
        out_shape=jax.ShapeDtypeStruct((t, d), F32),
        compiler_params=_params("parallel", "arbitrary"),
        name="out_proj",
    )(h, oa, ob, oc, wa, wb, wc)


HALO = 8


def _ffn_kernel(h_ref, halo_ref, g_ref, wg_ref, wu_ref, cw_ref, cb_ref, wd_ref, o_ref, c_scr, acc_scr,
                *, tiles_per_seq):
    i = pl.program_id(0)
    j = pl.program_id(1)
    tm = h_ref.shape[0]

    @pl.when(j == 0)
    def _():
        g = g_ref[...]
        c_scr[HALO:, :] = _rms(h_ref[...], g).astype(BF16)
        prev = jnp.where(i % tiles_per_seq == 0, 0.0, 1.0) * _rms(halo_ref[...], g)
        c_scr[:HALO, :] = prev.astype(BF16)
        acc_scr[...] = jnp.zeros_like(acc_scr)

    c_all = c_scr[...]
    hg = jnp.dot(c_all, wg_ref[...], preferred_element_type=F32)
    hu = jnp.dot(c_all[HALO:], wu_ref[...], preferred_element_type=F32)
    cw = cw_ref[...]
    conv = (hg[HALO - 2:HALO - 2 + tm] * cw[0:1] + hg[HALO - 1:HALO - 1 + tm] * cw[1:2]
            + hg[HALO:] * cw[2:3] + cb_ref[...])
    gelu = 0.5 * conv * (1.0 + jnp.tanh(0.7978845608028654 * (conv + 0.044715 * conv * conv * conv)))
    acc_scr[...] += jnp.dot((gelu * hu).astype(BF16), wd_ref[...], preferred_element_type=F32)

    @pl.when(j == pl.num_programs(1) - 1)
    def _():
        o_ref[...] = h_ref[...] + acc_scr[...]


def _ffn(h, g, w_gate, w_up, conv_w, conv_b, w_down, *, seq, tm, tf):
    t, d = h.shape
    ff = w_gate.shape[1]
    tm = min(tm, seq)
    cw = jnp.zeros((8, ff), F32).at[:conv_w.shape[0]].set(conv_w.astype(F32))
    return pl.pallas_call(
        functools.partial(_ffn_kernel, tiles_per_seq=seq // tm),
        grid=(t // tm, ff // tf),
        in_specs=[
            pl.BlockSpec((tm, d), lambda i, j: (i, 0)),
            pl.BlockSpec((HALO, d), lambda i, j: (jnp.maximum(i * (tm // HALO) - 1, 0), 0)),
            pl.BlockSpec((1, d), lambda i, j: (0, 0)),
            pl.BlockSpec((d, tf), lambda i, j: (0, j)),
            pl.BlockSpec((d, tf), lambda i, j: (0, j)),
            pl.BlockSpec((8, tf), lambda i, j: (0, j)),
            pl.BlockSpec((1, tf), lambda i, j: (0, j)),
            pl.BlockSpec((tf, d), lambda i, j: (j, 0)),
        ],
        out_specs=pl.BlockSpec((tm, d), lambda i, j: (i, 0)),
        out_shape=jax.ShapeDtypeStruct((t, d), F32),
        scratch_shapes=[pltpu.VMEM((HALO + tm, d), BF16), pltpu.VMEM((tm, d), F32)],
        compiler_params=_params("parallel", "arbitrary"),
        name="conv_ffn",
    )(h, h, g.reshape(1, d), w_gate, w_up, cw, conv_b.reshape(1, ff).astype(F32), w_down)


def _ple_kernel(h_ref, hcol_ref, g_ref, p_ref, wg_ref, wp_ref, o_ref, e_scr, p_scr):
    @pl.when(pl.program_id(1) == 0)
    def _():
        e_scr[...] = _rms(h_ref[...], g_ref[...]).astype(BF16)
        p_scr[...] = p_ref[...].astype(BF16)

    gate = _sigmoid(jnp.dot(e_scr[...], wg_ref[...], preferred_element_type=F32))
    emb = jnp.dot(p_scr[...], wp_ref[...], preferred_element_type=F32)
    o_ref[...] = hcol_ref[...] + gate * emb


def _ple(h, g, p, w_gate, w_proj, *, tm, tn):
    t, d = h.shape
    pd = p.shape[1]
    tm = min(tm, t)
    return pl.pallas_call(
        _ple_kernel,
        grid=(t // tm, d // tn),
        in_specs=[
            pl.BlockSpec((tm, d), lambda i, j: (i, 0)),
            pl.BlockSpec((tm, tn), lambda i, j: (i, j)),
            pl.BlockSpec((1, d), lambda i, j: (0, 0)),
            pl.BlockSpec((tm, pd), lambda i, j: (i, 0)),
            pl.BlockSpec((d, tn), lambda i, j: (0, j)),
            pl.BlockSpec((pd, tn), lambda i, j: (0, j)),
        ],
        out_specs=pl.BlockSpec((tm, tn), lambda i, j: (i, j)),
        out_shape=jax.ShapeDtypeStruct((t, d), F32),
        scratch_shapes=[pltpu.VMEM((tm, d), BF16), pltpu.VMEM((tm, pd), BF16)],
        compiler_params=_params("parallel", "arbitrary"),
        name="ple",
    )(h, h, g.reshape(1, d), p, w_gate, w_proj)


def _pad_w_in(w):
    d = w.shape[0]
    cut = CB_BF * HEAD_DIM
    w = w.astype(BF16)
    zeros = functools.partial(jnp.zeros, dtype=BF16)
    return jnp.concatenate(
        [w[:, :cut], w[:, cut:cut + B_HEADS], zeros((d, HEAD_DIM - B_HEADS)), w[:, cut + B_HEADS:],
         zeros((d, HEAD_DIM))], axis=1)


def _mixers(proj3, lb, hgrn_onorm, fox_bf, fox_qn, fox_kn, moba_qn, moba_kn):
    bsz, s, _ = proj3.shape
    oa = _hgrn(proj3, lb, hgrn_onorm, ts=512)

    fcum = _fox_fcum(proj3, fox_bf, ts=256)
    fk = jnp.transpose(fcum[:, :, :B_HEADS], (0, 2, 1))
    bq, bk, bv = _qkv_prep(proj3, fox_qn, fox_kn, CB_BQ, CB_BK, CB_BV, B_HEADS, ts=1024)
    ob = _fox_attn(bq, bk, bv, fk, tq=512, tk=256)

    cq, ck, cv, sel = _moba_prep(proj3, moba_qn, moba_kn)
    oc = _moba_attn(cq, ck, cv, sel)
    return oa, ob, oc


def kernel(x, p, attn_norm, w_in, fox_bf, lb_logits, hgrn_onorm, fox_qnorm, fox_knorm, moba_qnorm,
           moba_knorm, w_out, ffn_norm, w_gate, w_up, conv_w, conv_b, w_down, ple_norm, w_ple_gate,
           w_ple_proj):
    bsz, s, d = x.shape
    depth = w_in.shape[0]
    t = bsz * s
    lbs = _lower_bounds(lb_logits)
    h = x.reshape(t, d).astype(F32)
    for i in range(depth):
        proj = _norm_matmul(h, attn_norm[i], _pad_w_in(w_in[i]), tm=1024, tn=768)
        proj3 = proj.reshape(bsz, s, IN_COLS_PADDED)
        oa, ob, oc = _mixers(proj3, lbs[i], hgrn_onorm[i], fox_bf[i], fox_qnorm[i], fox_knorm[i],
                             moba_qnorm[i], moba_knorm[i])
        h = _out_proj(h, oa.reshape(t, A_W), ob.reshape(t, B_W), oc.reshape(t, C_W),
                      w_out[i].astype(BF16), tm=1024, tn=512)
        h = _ffn(h, ffn_norm[i], w_gate[i].astype(BF16), w_up[i].astype(BF16), conv_w[i], conv_b[i],
                 w_down[i].astype(BF16), seq=s, tm=512, tf=512)
        h = _ple(h, ple_norm[i], p[i].reshape(t, -1), w_ple_gate[i].astype(BF16),
                 w_ple_proj[i].astype(BF16), tm=1024, tn=512)
    return h.reshape(bsz, s, d).astype(x.dtype)
```

```python
import functools

import numpy as np
import jax
import jax.numpy as jnp
from jax import lax
from jax.experimental import pallas as pl
from jax.experimental.pallas import tpu as pltpu

F32 = jnp.float32
BF16 = jnp.bfloat16
HIGHEST = lax.Precision.HIGHEST

HEAD_DIM = 128
A_HEADS = 4
B_HEADS = 6
C_HEADS = 6
A_W = A_HEADS * HEAD_DIM
B_W = B_HEADS * HEAD_DIM
C_W = C_HEADS * HEAD_DIM
MOBA_BLOCK = 256
MOBA_TOPK = 3
EPS = 1e-6
TINY = 1e-30
NEG = -1e30
LOG2E = 1.4426950408889634
Q_SCALE = HEAD_DIM ** -0.5 * LOG2E

CB_AQ, CB_AF, CB_AI, CB_AG = 0, 4, 8, 12
CB_BQ, CB_BK, CB_BV, CB_BF = 16, 22, 28, 34
CB_CQ, CB_CK, CB_CV = 35, 41, 47
IN_COLS_PADDED = 54 * HEAD_DIM

HGRN_CHUNK = 64
FOX_HEADS_PER_STEP = 3
MOBA_HEADS_PER_STEP = 3
VMEM_LIMIT = 56 * 1024 * 1024

NT_DIMS = (((1,), (1,)), ((), ()))


def _params(*sem):
    return pltpu.CompilerParams(dimension_semantics=sem, vmem_limit_bytes=VMEM_LIMIT)


def _sigmoid(x):
    return 1.0 / (1.0 + jnp.exp(-x))


def _rms(x, g):
    ms = jnp.mean(x * x, axis=-1, keepdims=True)
    return (x * lax.rsqrt(ms + EPS)) * g


def _lower_bounds_kernel(x_ref, o_ref):
    x = x_ref[...]
    depth = x.shape[0]
    m = x[0:1]
    for i in range(1, depth):
        m = jnp.maximum(m, x[i:i + 1])
    e = jnp.exp(x - m)
    tot = e[0:1]
    for i in range(1, depth):
        tot = tot + e[i:i + 1]
    sm = e / tot
    run = jnp.zeros_like(m)
    for i in range(depth):
        run = run + sm[i:i + 1]
        o_ref[i:i + 1, :] = run - sm[0:1]


def _lower_bounds(lb_logits):
    return pl.pallas_call(
        _lower_bounds_kernel,
        out_shape=jax.ShapeDtypeStruct(lb_logits.shape, F32),
    )(lb_logits.astype(F32))


def _norm_matmul_kernel(x_ref, g_ref, w_ref, o_ref, a_scr):
    @pl.when(pl.program_id(1) == 0)
    def _():
        a_scr[...] = _rms(x_ref[...], g_ref[...]).astype(BF16)

    o_ref[...] = jnp.dot(a_scr[...], w_ref[...], preferred_element_type=F32).astype(o_ref.dtype)


def _norm_matmul(x, g, w, *, tm, tn):
    t, k = x.shape
    n = w.shape[1]
    tm = min(tm, t)
    return pl.pallas_call(
        _norm_matmul_kernel,
        grid=(t // tm, n // tn),
        in_specs=[
            pl.BlockSpec((tm, k), lambda i, j: (i, 0)),
            pl.BlockSpec((1, k), lambda i, j: (0, 0)),
            pl.BlockSpec((k, tn), lambda i, j: (0, j)),
        ],
        out_specs=pl.BlockSpec((tm, tn), lambda i, j: (i, j)),
        out_shape=jax.ShapeDtypeStruct((t, n), F32),
        scratch_shapes=[pltpu.VMEM((tm, k), BF16)],
        compiler_params=_params("parallel", "arbitrary"),
        name="norm_in_proj",
    )(x, g.reshape(1, k), w)


def _hgrn_levels(chunk):
    levels = []
    m = chunk // 2
    while m >= 1:
        levels.append(m)
        m //= 2
    return levels


def _hgrn_masks(chunk):
    t = np.arange(chunk)[:, None]
    s = np.arange(chunk)[None, :]
    out = []
    for m in _hgrn_levels(chunk):
        same_parent = (t // (2 * m)) == (s // (2 * m))
        out.append((same_parent & ((t // m) % 2 == 1) & ((s // m) % 2 == 0)).astype(np.float32))
    return np.stack(out)


def _hgrn_kernel(q_ref, f_ref, i_ref, g_ref, lb_ref, on_ref, tri_ref, mask_ref, o_ref, st_scr, b_scr,
                 *, chunk, n_chunks):
    @pl.when(pl.program_id(2) == 0)
    def _():
        st_scr[...] = jnp.zeros_like(st_scr)

    lb = lb_ref[...]
    one_m_lb = 1.0 - lb
    onorm = on_ref[...]
    tri = tri_ref[...]
    row = lax.broadcasted_iota(jnp.int32, (chunk, HEAD_DIM), 0)

    def boundary_rows(m):
        if 2 * m >= 8:
            pieces = [
                jnp.broadcast_to(b_scr[2 * m * p + m - 1:2 * m * p + m, :], (2 * m, HEAD_DIM))
                for p in range(chunk // (2 * m))
            ]
            return pieces[0] if len(pieces) == 1 else jnp.concatenate(pieces, axis=0)
        assert m == 2
        lo = jnp.concatenate(
            [jnp.broadcast_to(b_scr[8 * p + 1:8 * p + 2, :], (8, HEAD_DIM)) for p in range(chunk // 8)], axis=0)
        hi = jnp.concatenate(
            [jnp.broadcast_to(b_scr[8 * p + 5:8 * p + 6, :], (8, HEAD_DIM)) for p in range(chunk // 8)], axis=0)
        return jnp.where((row & 4) == 0, lo, hi)

    def one_chunk(c, carry):
        rows = pl.ds(pl.multiple_of(c * chunk, chunk), chunk)
        z = f_ref[rows, :]
        f = lb + one_m_lb * _sigmoid(z)
        logf = jnp.log(jnp.maximum(f, TINY))
        k = one_m_lb * _sigmoid(-z)
        qraw = q_ref[rows, :]
        q = qraw * _sigmoid(qraw)
        v = i_ref[rows, :]
        v16 = v.astype(BF16)

        b = jnp.dot(tri, logf, precision=HIGHEST, preferred_element_type=F32)
        b_scr[...] = b

        attn = jnp.zeros((chunk, chunk), F32)
        for lev, m in enumerate(_hgrn_levels(chunk)):
            right = (row & m) != 0
            if m == 1:
                arg = jnp.where(right, logf, 0.0)
            else:
                d = b - boundary_rows(m)
                arg = jnp.where(right, d, -d)
            e = jnp.exp(arg)
            qt = jnp.where(right, q * e, 0.0).astype(BF16)
            kt = jnp.where(right, 0.0, k * e).astype(BF16)
            attn = attn + mask_ref[lev] * lax.dot_general(qt, kt, NT_DIMS, preferred_element_type=F32)
        diag = jnp.sum(q * k, axis=-1, keepdims=True)
        o = jnp.dot(attn.astype(BF16), v16, preferred_element_type=F32) + diag * v

        st = st_scr[...]
        qe = (q * jnp.exp(b)).astype(BF16)
        o = o + lax.dot_general(qe, st.astype(BF16), NT_DIMS, preferred_element_type=F32)

        b_last = b_scr[chunk - 1:chunk, :]
        k_dec = (k * jnp.exp(b_last - b)).astype(BF16)
        upd = jnp.dot(v.T.astype(BF16), k_dec, preferred_element_type=F32)
        st_scr[...] = st * jnp.exp(b_last) + upd

        gate = g_ref[rows, :]
        o_ref[rows, :] = (_rms(o, onorm) * (gate * _sigmoid(gate))).astype(o_ref.dtype)
        return carry

    lax.fori_loop(0, n_chunks, one_chunk, 0)


def _hgrn(proj3, lb, onorm, *, ts):
    bsz, s, _ = proj3.shape
    ts = min(ts, s)
    chunk = HGRN_CHUNK
    tri = jnp.asarray(np.tril(np.ones((chunk, chunk), np.float32)))
    masks = jnp.asarray(_hgrn_masks(chunk))
    nlev = masks.shape[0]

    def col(cb):
        return pl.BlockSpec((None, ts, HEAD_DIM), lambda b, h, i: (b, i, cb + h))

    return pl.pallas_call(
        functools.partial(_hgrn_kernel, chunk=chunk, n_chunks=ts // chunk),
        grid=(bsz, A_HEADS, s // ts),
        in_specs=[
            col(CB_AQ), col(CB_AF), col(CB_AI), col(CB_AG),
            pl.BlockSpec((1, HEAD_DIM), lambda b, h, i: (0, h)),
            pl.BlockSpec((1, HEAD_DIM), lambda b, h, i: (0, 0)),
            pl.BlockSpec((chunk, chunk), lambda b, h, i: (0, 0)),
            pl.BlockSpec((nlev, chunk, chunk), lambda b, h, i: (0, 0, 0)),
        ],
        out_specs=pl.BlockSpec((None, ts, HEAD_DIM), lambda b, h, i: (b, i, h)),
        out_shape=jax.ShapeDtypeStruct((bsz, s, A_W), BF16),
        scratch_shapes=[pltpu.VMEM((HEAD_DIM, HEAD_DIM), F32), pltpu.VMEM((chunk, HEAD_DIM), F32)],
        compiler_params=_params("parallel", "parallel", "arbitrary"),
        name="hgrn2",
    )(proj3, proj3, proj3, proj3, lb.reshape(1, A_W), onorm.reshape(1, HEAD_DIM), tri, masks)


ATT_BLOCK = 256


def _store_transposed_blocks(o_ref, x):
    for j in range(x.shape[0] // ATT_BLOCK):
        o_ref[j] = x[j * ATT_BLOCK:(j + 1) * ATT_BLOCK].T.astype(o_ref.dtype)


def _fox_prep_kernel(q_ref, k_ref, v_ref, fc_ref, qg_ref, kg_ref, qo_ref, ko_ref, vo_ref, fo_ref):
    _store_transposed_blocks(qo_ref, _rms(q_ref[...], qg_ref[...]) * Q_SCALE)
    ko_ref[...] = _rms(k_ref[...], kg_ref[...]).astype(BF16)
    _store_transposed_blocks(vo_ref, v_ref[...])
    fc = fc_ref[...]
    lane = lax.broadcasted_iota(jnp.int32, fc.shape, 1)
    mine = jnp.sum(jnp.where(lane == pl.program_id(1), fc, 0.0), axis=-1, keepdims=True)
    fo_ref[...] = jnp.broadcast_to(mine, fc.shape)


def _fox_prep(proj3, fcum, qg, kg, *, ts):
    bsz, s, _ = proj3.shape
    ts = min(ts, s)
    nblk = ts // ATT_BLOCK

    def col(cb):
        return pl.BlockSpec((None, ts, HEAD_DIM), lambda b, h, i: (b, i, cb + h))

    gain = pl.BlockSpec((1, HEAD_DIM), lambda b, h, i: (0, 0))
    rows = pl.BlockSpec((None, None, ts, HEAD_DIM), lambda b, h, i: (b, h, i, 0))
    tblk = pl.BlockSpec((None, None, nblk, HEAD_DIM, ATT_BLOCK), lambda b, h, i: (b, h, i, 0, 0))
    rows_shape = (bsz, B_HEADS, s, HEAD_DIM)
    tblk_shape = jax.ShapeDtypeStruct((bsz, B_HEADS, s // ATT_BLOCK, HEAD_DIM, ATT_BLOCK), BF16)
    return pl.pallas_call(
        _fox_prep_kernel,
        grid=(bsz, B_HEADS, s // ts),
        in_specs=[col(CB_BQ), col(CB_BK), col(CB_BV),
                  pl.BlockSpec((None, ts, HEAD_DIM), lambda b, h, i: (b, i, 0)), gain, gain],
        out_specs=[tblk, rows, tblk, rows],
        out_shape=[tblk_shape, jax.ShapeDtypeStruct(rows_shape, BF16), tblk_shape,
                   jax.ShapeDtypeStruct(rows_shape, F32)],
        compiler_params=_params("parallel", "parallel", "parallel"),
        name="fox_prep",
    )(proj3, proj3, proj3, fcum, qg.reshape(1, HEAD_DIM), kg.reshape(1, HEAD_DIM))


def _fox_fcum_kernel(f_ref, bf_ref, tri_ref, o_ref, carry_scr):
    @pl.when(pl.program_id(1) == 0)
    def _():
        carry_scr[...] = jnp.zeros_like(carry_scr)

    x = f_ref[...] + bf_ref[...]
    logf = jnp.minimum(x, 0.0) - jnp.log(1.0 + jnp.exp(-jnp.abs(x)))
    cum = jnp.dot(tri_ref[...], logf, precision=HIGHEST, preferred_element_type=F32) + carry_scr[...]
    o_ref[...] = cum * LOG2E
    carry_scr[...] = cum[cum.shape[0] - 1:, :]


def _fox_fcum(proj3, b_f, *, ts):
    bsz, s, _ = proj3.shape
    ts = min(ts, s)
    tri = jnp.asarray(np.tril(np.ones((ts, ts), np.float32)))
    bf_pad = jnp.zeros((1, HEAD_DIM), F32).at[0, :B_HEADS].set(b_f.astype(F32))
    return pl.pallas_call(
        _fox_fcum_kernel,
        grid=(bsz, s // ts),
        in_specs=[
            pl.BlockSpec((None, ts, HEAD_DIM), lambda b, i: (b, i, CB_BF)),
            pl.BlockSpec((1, HEAD_DIM), lambda b, i: (0, 0)),
            pl.BlockSpec((ts, ts), lambda b, i: (0, 0)),
        ],
        out_specs=pl.BlockSpec((None, ts, HEAD_DIM), lambda b, i: (b, i, 0)),
        out_shape=jax.ShapeDtypeStruct((bsz, s, HEAD_DIM), F32),
        scratch_shapes=[pltpu.VMEM((1, HEAD_DIM), F32)],
        compiler_params=_params("parallel", "arbitrary"),
        name="fox_fcum",
    )(proj3, bf_pad, tri)


def _flash_update(carry, s, vt, m_new=None, sub=None):
    m, l, acc = carry
    if m_new is None:
        m_new = jnp.maximum(m, jnp.max(s, axis=0, keepdims=True))
        sub = m_new
    p = jnp.exp2(s - sub)
    alpha = jnp.exp2(m - m_new)
    l = alpha * l + jnp.sum(p, axis=0, keepdims=True)
    acc = alpha * acc + jnp.dot(vt, p.astype(BF16), preferred_element_type=F32)
    return m_new, l, acc


def _flash_init():
    return (jnp.full((1, ATT_BLOCK), NEG, F32), jnp.zeros((1, ATT_BLOCK), F32),
            jnp.zeros((HEAD_DIM, ATT_BLOCK), F32))


def _flash_store(o_ref, h, carry):
    _, l, acc = carry
    o_ref[:, h * HEAD_DIM:(h + 1) * HEAD_DIM] = (acc / l).T.astype(o_ref.dtype)


def _key_rows(kj):
    return pl.ds(pl.multiple_of(kj * ATT_BLOCK, ATT_BLOCK), ATT_BLOCK)


def _fox_attn_kernel(q_ref, k_ref, v_ref, fk_ref, o_ref, *, hp):
    qi = pl.program_id(2)

    def scores(h, kj):
        bias = fk_ref[h, _key_rows(kj), :]
        s = jnp.dot(k_ref[h, _key_rows(kj), :], q_ref[h], preferred_element_type=F32)
        return s - jnp.concatenate([bias] * (ATT_BLOCK // HEAD_DIM), axis=1)

    def past(kj, carries):
        ss = [scores(h, kj) for h in range(hp)]
        return tuple(_flash_update(carries[h], ss[h], v_ref[h, kj]) for h in range(hp))

    carries = lax.fori_loop(0, qi, past, tuple(_flash_init() for _ in range(hp)))
    kpos = lax.broadcasted_iota(jnp.int32, (ATT_BLOCK, ATT_BLOCK), 0)
    qpos = lax.broadcasted_iota(jnp.int32, (ATT_BLOCK, ATT_BLOCK), 1)
    for h in range(hp):
        s = jnp.where(kpos <= qpos, scores(h, qi), NEG)
        _flash_store(o_ref, h, _flash_update(carries[h], s, v_ref[h, qi]))


def _att_specs(s, hp):
    nblk = s // ATT_BLOCK
    q_tile = pl.BlockSpec((None, hp, None, HEAD_DIM, ATT_BLOCK), lambda b, h, i: (b, h, i, 0, 0))
    k_full = pl.BlockSpec((None, hp, s, HEAD_DIM), lambda b, h, i: (b, h, 0, 0))
    v_full = pl.BlockSpec((None, hp, nblk, HEAD_DIM, ATT_BLOCK), lambda b, h, i: (b, h, 0, 0, 0))
    out = pl.BlockSpec((None, ATT_BLOCK, hp * HEAD_DIM), lambda b, h, i: (b, i, h))
    return q_tile, k_full, v_full, out


def _fox_attn(qt, kn, vt, fkb, *, hp):
    bsz, nh, s, _ = kn.shape
    q_tile, k_full, v_full, out = _att_specs(s, hp)
    return pl.pallas_call(
        functools.partial(_fox_attn_kernel, hp=hp),
        grid=(bsz, nh // hp, s // ATT_BLOCK),
        in_specs=[q_tile, k_full, v_full, k_full],
        out_specs=out,
        out_shape=jax.ShapeDtypeStruct((bsz, s, nh * HEAD_DIM), BF16),
        compiler_params=_params("parallel", "parallel", "arbitrary"),
        name="fox_attn",
    )(qt, kn, vt, fkb)


def _moba_prep_kernel(q_ref, k_ref, v_ref, qg_ref, kg_ref, qo_ref, ko_ref, vo_ref, sel_ref, *, topk):
    qn = _rms(q_ref[...], qg_ref[...])
    kn = _rms(k_ref[...], kg_ref[...])
    _store_transposed_blocks(qo_ref, qn * Q_SCALE)
    ko_ref[...] = kn.astype(BF16)
    _store_transposed_blocks(vo_ref, v_ref[...])

    s = qn.shape[0]
    nb = s // MOBA_BLOCK
    nbp = sel_ref.shape[0]
    kbar = jnp.sum(kn.reshape(nb, MOBA_BLOCK, HEAD_DIM), axis=1) * (1.0 / MOBA_BLOCK)
    if nbp > nb:
        kbar = jnp.concatenate([kbar, jnp.zeros((nbp - nb, HEAD_DIM), F32)], axis=0)
    gate = lax.dot_general(kbar, qn, NT_DIMS, precision=HIGHEST, preferred_element_type=F32)
    blk = lax.broadcasted_iota(jnp.int32, (nbp, s), 0)
    own = lax.broadcasted_iota(jnp.int32, (nbp, s), 1) // MOBA_BLOCK
    gate = jnp.where(blk < own, gate, NEG)
    sel = jnp.zeros((nbp, s), F32)
    for _ in range(topk):
        mx = jnp.max(gate, axis=0, keepdims=True)
        idx = jnp.min(jnp.where(gate == mx, blk, nbp), axis=0, keepdims=True)
        hit = blk == idx
        sel = jnp.where(hit & (mx > NEG / 2), 1.0, sel)
        gate = jnp.where(hit, -jnp.inf, gate)
    sel_ref[...] = sel


def _moba_prep(proj3, qg, kg):
    bsz, s, _ = proj3.shape
    nb = s // MOBA_BLOCK
    nbp = -(-nb // 8) * 8
    topk = min(MOBA_TOPK, nb - 1)

    def col(cb):
        return pl.BlockSpec((None, s, HEAD_DIM), lambda b, h: (b, 0, cb + h))

    gain = pl.BlockSpec((1, HEAD_DIM), lambda b, h: (0, 0))
    rows = pl.BlockSpec((None, None, s, HEAD_DIM), lambda b, h: (b, h, 0, 0))
    tblk = pl.BlockSpec((None, None, s // ATT_BLOCK, HEAD_DIM, ATT_BLOCK), lambda b, h: (b, h, 0, 0, 0))
    tblk_shape = jax.ShapeDtypeStruct((bsz, C_HEADS, s // ATT_BLOCK, HEAD_DIM, ATT_BLOCK), BF16)
    return pl.pallas_call(
        functools.partial(_moba_prep_kernel, topk=topk),
        grid=(bsz, C_HEADS),
        in_specs=[col(CB_CQ), col(CB_CK), col(CB_CV), gain, gain],
        out_specs=[tblk, rows, tblk, pl.BlockSpec((None, None, nbp, s), lambda b, h: (b, h, 0, 0))],
        out_shape=[tblk_shape, jax.ShapeDtypeStruct((bsz, C_HEADS, s, HEAD_DIM), BF16), tblk_shape,
                   jax.ShapeDtypeStruct((bsz, C_HEADS, nbp, s), F32)],
        compiler_params=_params("parallel", "parallel"),
        name="moba_prep",
    )(proj3, proj3, proj3, qg.reshape(1, HEAD_DIM), kg.reshape(1, HEAD_DIM))


def _moba_attn_kernel(q_ref, k_ref, v_ref, sel_ref, slope_ref, o_ref, *, hp):
    assert ATT_BLOCK == MOBA_BLOCK
    qi = pl.program_id(2)
    rel = (lax.broadcasted_iota(jnp.int32, (ATT_BLOCK, ATT_BLOCK), 0)
           - lax.broadcasted_iota(jnp.int32, (ATT_BLOCK, ATT_BLOCK), 1))
    relf = rel.astype(F32)
    slopes = [slope_ref[h, 0:1, 0:1] for h in range(hp)]
    alibi = [slopes[h] * relf for h in range(hp)]

    def scores(h, kj):
        return jnp.dot(k_ref[h, _key_rows(kj), :], q_ref[h], preferred_element_type=F32) + alibi[h]

    def past(kj, carries):
        dist = ((kj - qi) * ATT_BLOCK).astype(F32)
        ss = [scores(h, kj) for h in range(hp)]
        out = []
        for h in range(hp):
            s = ss[h]
            chosen = sel_ref[h, pl.ds(kj, 1), :] > 0.5
            shift = slopes[h] * dist
            blk_max = jnp.max(s, axis=0, keepdims=True) + shift
            m_new = jnp.maximum(carries[h][0], jnp.where(chosen, blk_max, NEG))
            sub = jnp.where(chosen, m_new - shift, jnp.inf)
            out.append(_flash_update(carries[h], s, v_ref[h, kj], m_new=m_new, sub=sub))
        return tuple(out)

    carries = lax.fori_loop(0, qi, past, tuple(_flash_init() for _ in range(hp)))
    for h in range(hp):
        s = jnp.where(rel <= 0, scores(h, qi), NEG)
        _flash_store(o_ref, h, _flash_update(carries[h], s, v_ref[h, qi]))


def _moba_attn(qt, kn, vt, sel, *, hp):
    bsz, nh, s, _ = kn.shape
    nbp = sel.shape[2]
    slopes = np.exp2(-8.0 * np.arange(1, nh + 1, dtype=np.float32) / nh).astype(np.float32) * LOG2E
    slopes = jnp.asarray(np.broadcast_to(slopes[:, None, None], (nh, 8, HEAD_DIM)).copy())
    q_tile, k_full, v_full, out = _att_specs(s, hp)
    return pl.pallas_call(
        functools.partial(_moba_attn_kernel, hp=hp),
        grid=(bsz, nh // hp, s // ATT_BLOCK),
        in_specs=[q_tile, k_full, v_full,
                  pl.BlockSpec((None, hp, nbp, ATT_BLOCK), lambda b, h, i: (b, h, 0, i)),
                  pl.BlockSpec((hp, 8, HEAD_DIM), lambda b, h, i: (h, 0, 0))],
        out_specs=out,
        out_shape=jax.ShapeDtypeStruct((bsz, s, nh * HEAD_DIM), BF16),
        compiler_params=_params("parallel", "parallel", "arbitrary"),
        name="moba_attn",
    )(qt, kn, vt, sel, slopes)


def _out_proj_kernel(h_ref, a_ref, b_ref, c_ref, wa_ref, wb_ref, wc_ref, o_ref):
    acc = jnp.dot(a_ref[...], wa_ref[...], preferred_element_type=F32)
    acc = acc + jnp.dot(b_ref[...], wb_ref[...], preferred_element_type=F32)
    acc = acc + jnp.dot(c_ref[...], wc_ref[...], preferred_element_type=F32)
    o_ref[...] = h_ref[...] + acc


def _out_proj(h, oa, ob, oc, w_out, *, tm, tn):
    t, d = h.shape
    tm = min(tm, t)
    wa, wb, wc = w_out[:A_W], w_out[A_W:A_W + B_W], w_out[A_W + B_W:]

    def act(w):
        return pl.BlockSpec((tm, w), lambda i, j: (i, 0))

    def wgt(w):
        return pl.BlockSpec((w, tn), lambda i, j: (0, j))

    return pl.pallas_call(
        _out_proj_kernel,
        grid=(t // tm, d // tn),
        in_specs=[pl.BlockSpec((tm, tn), lambda i, j: (i, j)), act(A_W), act(B_W), act(C_W),
                  wgt(A_W), wgt(B_W), wgt(C_W)],
        out_specs=pl.BlockSpec((tm, tn), lambda i, j: (i, j)),
        out_shape=jax.ShapeDtypeStruct((t, d), F32),
        compiler_params=_params("parallel", "arbitrary"),
        name="out_proj",
    )(h, oa, ob, oc, wa, wb, wc)


HALO = 8


def _ffn_kernel(h_ref, halo_ref, g_ref, wg_ref, wu_ref, cw_ref, cb_ref, wd_ref, o_ref, c_scr, acc_scr,
                *, tiles_per_seq):
    i = pl.program_id(0)
    j = pl.program_id(1)
    tm = h_ref.shape[0]

    @pl.when(j == 0)
    def _():
        g = g_ref[...]
        c_scr[HALO:, :] = _rms(h_ref[...], g).astype(BF16)
        prev = jnp.where(i % tiles_per_seq == 0, 0.0, 1.0) * _rms(halo_ref[...], g)
        c_scr[:HALO, :] = prev.astype(BF16)
        acc_scr[...] = jnp.zeros_like(acc_scr)

    c_all = c_scr[...]
    hg = jnp.dot(c_all, wg_ref[...], preferred_element_type=F32)
    hu = jnp.dot(c_all[HALO:], wu_ref[...], preferred_element_type=F32)
    cw = cw_ref[...]
    conv = (hg[HALO - 2:HALO - 2 + tm] * cw[0:1] + hg[HALO - 1:HALO - 1 + tm] * cw[1:2]
            + hg[HALO:] * cw[2:3] + cb_ref[...])
    gelu = 0.5 * conv * (1.0 + jnp.tanh(0.7978845608028654 * (conv + 0.044715 * conv * conv * conv)))
    acc_scr[...] += jnp.dot((gelu * hu).astype(BF16), wd_ref[...], preferred_element_type=F32)

    @pl.when(j == pl.num_programs(1) - 1)
    def _():
        o_ref[...] = h_ref[...] + acc_scr[...]


def _ffn(h, g, w_gate, w_up, conv_w, conv_b, w_down, *, seq, tm, tf):
    t, d = h.shape
    ff = w_gate.shape[1]
    tm = min(tm, seq)
    cw = jnp.zeros((8, ff), F32).at[:conv_w.shape[0]].set(conv_w.astype(F32))
    return pl.pallas_call(
        functools.partial(_ffn_kernel, tiles_per_seq=seq // tm),
        grid=(t // tm, ff // tf),
        in_specs=[
            pl.BlockSpec((tm, d), lambda i, j: (i, 0)),
            pl.BlockSpec((HALO, d), lambda i, j: (jnp.maximum(i * (tm // HALO) - 1, 0), 0)),
            pl.BlockSpec((1, d), lambda i, j: (0, 0)),
            pl.BlockSpec((d, tf), lambda i, j: (0, j)),
            pl.BlockSpec((d, tf), lambda i, j: (0, j)),
            pl.BlockSpec((8, tf), lambda i, j: (0, j)),
            pl.BlockSpec((1, tf), lambda i, j: (0, j)),
            pl.BlockSpec((tf, d), lambda i, j: (j, 0)),
        ],
        out_specs=pl.BlockSpec((tm, d), lambda i, j: (i, 0)),
        out_shape=jax.ShapeDtypeStruct((t, d), F32),
        scratch_shapes=[pltpu.VMEM((HALO + tm, d), BF16), pltpu.VMEM((tm, d), F32)],
        compiler_params=_params("parallel", "arbitrary"),
        name="conv_ffn",
    )(h, h, g.reshape(1, d), w_gate, w_up, cw, conv_b.reshape(1, ff).astype(F32), w_down)


def _ple_kernel(h_ref, hcol_ref, g_ref, p_ref, wg_ref, wp_ref, o_ref, e_scr, p_scr):
    @pl.when(pl.program_id(1) == 0)
    def _():
        e_scr[...] = _rms(h_ref[...], g_ref[...]).astype(BF16)
        p_scr[...] = p_ref[...].astype(BF16)

    gate = _sigmoid(jnp.dot(e_scr[...], wg_ref[...], preferred_element_type=F32))
    emb = jnp.dot(p_scr[...], wp_ref[...], preferred_element_type=F32)
    o_ref[...] = hcol_ref[...] + gate * emb


def _ple(h, g, p, w_gate, w_proj, *, tm, tn):
    t, d = h.shape
    pd = p.shape[1]
    tm = min(tm, t)
    return pl.pallas_call(
        _ple_kernel,
        grid=(t // tm, d // tn),
        in_specs=[
            pl.BlockSpec((tm, d), lambda i, j: (i, 0)),
            pl.BlockSpec((tm, tn), lambda i, j: (i, j)),
            pl.BlockSpec((1, d), lambda i, j: (0, 0)),
            pl.BlockSpec((tm, pd), lambda i, j: (i, 0)),
            pl.BlockSpec((d, tn), lambda i, j: (0, j)),
            pl.BlockSpec((pd, tn), lambda i, j: (0, j)),
        ],
        out_specs=pl.BlockSpec((tm, tn), lambda i, j: (i, j)),
        out_shape=jax.ShapeDtypeStruct((t, d), F32),
        scratch_shapes=[pltpu.VMEM((tm, d), BF16), pltpu.VMEM((tm, pd), BF16)],
        compiler_params=_params("parallel", "arbitrary"),
        name="ple",
    )(h, h, g.reshape(1, d), p, w_gate, w_proj)


def _pad_w_in(w):
    d = w.shape[0]
    cut = CB_BF * HEAD_DIM
    w = w.astype(BF16)
    zeros = functools.partial(jnp.zeros, dtype=BF16)
    return jnp.concatenate(
        [w[:, :cut], w[:, cut:cut + B_HEADS], zeros((d, HEAD_DIM - B_HEADS)), w[:, cut + B_HEADS:],
         zeros((d, HEAD_DIM))], axis=1)


def _mixers(proj3, lb, hgrn_onorm, fox_bf, fox_qn, fox_kn, moba_qn, moba_kn):
    bsz, s, _ = proj3.shape
    oa = _hgrn(proj3, lb, hgrn_onorm, ts=512)

    fcum = _fox_fcum(proj3, fox_bf, ts=256)
    bq, bk, bv, fkb = _fox_prep(proj3, fcum, fox_qn, fox_kn, ts=1024)
    ob = _fox_attn(bq, bk, bv, fkb, hp=FOX_HEADS_PER_STEP)

    cq, ck, cv, sel = _moba_prep(proj3, moba_qn, moba_kn)
    oc = _moba_attn(cq, ck, cv, sel, hp=MOBA_HEADS_PER_STEP)
    return oa, ob, oc


def kernel(x, p, attn_norm, w_in, fox_bf, lb_logits, hgrn_onorm, fox_qnorm, fox_knorm, moba_qnorm,
           moba_knorm, w_out, ffn_norm, w_gate, w_up, conv_w, conv_b, w_down, ple_norm, w_ple_gate,
           w_ple_proj):
    bsz, s, d = x.shape
    depth = w_in.shape[0]
    t = bsz * s
    lbs = _lower_bounds(lb_logits)
    h = x.reshape(t, d).astype(F32)
    for i in range(depth):
        proj = _norm_matmul(h, attn_norm[i], _pad_w_in(w_in[i]), tm=1024, tn=768)
        proj3 = proj.reshape(bsz, s, IN_COLS_PADDED)
        oa, ob, oc = _mixers(proj3, lbs[i], hgrn_onorm[i], fox_bf[i], fox_qnorm[i], fox_knorm[i],
                             moba_qnorm[i], moba_knorm[i])
        h = _out_proj(h, oa.reshape(t, A_W), ob.reshape(t, B_W), oc.reshape(t, C_W),
                      w_out[i].astype(BF16), tm=1024, tn=512)
        h = _ffn(h, ffn_norm[i], w_gate[i].astype(BF16), w_up[i].astype(BF16), conv_w[i], conv_b[i],
                 w_down[i].astype(BF16), seq=s, tm=512, tf=512)
        h = _ple(h, ple_norm[i], p[i].reshape(t, -1), w_ple_gate[i].astype(BF16),
                 w_ple_proj[i].astype(BF16), tm=1024, tn=512)
    return h.reshape(bsz, s, d).astype(x.dtype)
```

```python
import functools

import numpy as np
import jax
import jax.numpy as jnp
from jax import lax
from jax.experimental import pallas as pl
from jax.experimental.pallas import tpu as pltpu

F32 = jnp.float32
BF16 = jnp.bfloat16
HIGHEST = lax.Precision.HIGHEST

HEAD_DIM = 128
A_HEADS = 4
B_HEADS = 6
C_HEADS = 6
A_W = A_HEADS * HEAD_DIM
B_W = B_HEADS * HEAD_DIM
C_W = C_HEADS * HEAD_DIM
MOBA_BLOCK = 256
MOBA_TOPK = 3
EPS = 1e-6
TINY = 1e-30
NEG = -1e30
LOG2E = 1.4426950408889634
Q_SCALE = HEAD_DIM ** -0.5 * LOG2E

CB_AQ, CB_AF, CB_AI, CB_AG = 0, 4, 8, 12
CB_BQ, CB_BK, CB_BV, CB_BF = 16, 22, 28, 34
CB_CQ, CB_CK, CB_CV = 35, 41, 47
IN_COLS_PADDED = 54 * HEAD_DIM

HGRN_CHUNK = 64
FOX_HEADS_PER_STEP = 6
MOBA_HEADS_PER_STEP = 6
VMEM_LIMIT = 56 * 1024 * 1024

NT_DIMS = (((1,), (1,)), ((), ()))


def _params(*sem):
    return pltpu.CompilerParams(dimension_semantics=sem, vmem_limit_bytes=VMEM_LIMIT)


def _sigmoid(x):
    return 1.0 / (1.0 + jnp.exp(-x))


def _rms(x, g):
    ms = jnp.mean(x * x, axis=-1, keepdims=True)
    return (x * lax.rsqrt(ms + EPS)) * g


def _lower_bounds_kernel(x_ref, o_ref):
    x = x_ref[...]
    depth = x.shape[0]
    m = x[0:1]
    for i in range(1, depth):
        m = jnp.maximum(m, x[i:i + 1])
    e = jnp.exp(x - m)
    tot = e[0:1]
    for i in range(1, depth):
        tot = tot + e[i:i + 1]
    sm = e / tot
    run = jnp.zeros_like(m)
    for i in range(depth):
        run = run + sm[i:i + 1]
        o_ref[i:i + 1, :] = run - sm[0:1]


def _lower_bounds(lb_logits):
    return pl.pallas_call(
        _lower_bounds_kernel,
        out_shape=jax.ShapeDtypeStruct(lb_logits.shape, F32),
    )(lb_logits.astype(F32))


def _norm_matmul_kernel(x_ref, g_ref, w_ref, o_ref, a_scr):
    @pl.when(pl.program_id(1) == 0)
    def _():
        a_scr[...] = _rms(x_ref[...], g_ref[...]).astype(BF16)

    o_ref[...] = jnp.dot(a_scr[...], w_ref[...], preferred_element_type=F32).astype(o_ref.dtype)


def _norm_matmul(x, g, w, layer, *, tm, tn):
    t, k = x.shape
    n = w.shape[2]
    tm = min(tm, t)
    return pl.pallas_call(
        _norm_matmul_kernel,
        grid=(t // tm, n // tn),
        in_specs=[
            pl.BlockSpec((tm, k), lambda i, j: (i, 0)),
            pl.BlockSpec((1, k), lambda i, j: (0, 0)),
            pl.BlockSpec((None, k, tn), lambda i, j: (layer, 0, j)),
        ],
        out_specs=pl.BlockSpec((tm, tn), lambda i, j: (i, j)),
        out_shape=jax.ShapeDtypeStruct((t, n), F32),
        scratch_shapes=[pltpu.VMEM((tm, k), BF16)],
        compiler_params=_params("parallel", "arbitrary"),
        name="norm_in_proj",
    )(x, g.reshape(1, k), w)


def _hgrn_levels(chunk):
    levels = []
    m = chunk // 2
    while m >= 1:
        levels.append(m)
        m //= 2
    return levels


def _hgrn_masks(chunk):
    t = np.arange(chunk)[:, None]
    s = np.arange(chunk)[None, :]
    out = []
    for m in _hgrn_levels(chunk):
        same_parent = (t // (2 * m)) == (s // (2 * m))
        out.append((same_parent & ((t // m) % 2 == 1) & ((s // m) % 2 == 0)).astype(np.float32))
    return np.stack(out)


def _hgrn_kernel(q_ref, f_ref, i_ref, g_ref, lb_ref, on_ref, tri_ref, mask_ref, o_ref, st_scr, b_scr,
                 *, chunk, n_chunks):
    @pl.when(pl.program_id(1) == 0)
    def _():
        st_scr[...] = jnp.zeros_like(st_scr)

    width = A_W
    lb = lb_ref[...]
    one_m_lb = 1.0 - lb
    onorm = on_ref[...]
    tri = tri_ref[...]
    row = lax.broadcasted_iota(jnp.int32, (chunk, width), 0)
    levels = _hgrn_levels(chunk)
    heads = [slice(h * HEAD_DIM, (h + 1) * HEAD_DIM) for h in range(A_HEADS)]

    def boundary_rows(m):
        if 2 * m >= 8:
            pieces = [
                jnp.broadcast_to(b_scr[2 * m * p + m - 1:2 * m * p + m, :], (2 * m, width))
                for p in range(chunk // (2 * m))
            ]
            return pieces[0] if len(pieces) == 1 else jnp.concatenate(pieces, axis=0)
        assert m == 2
        lo = jnp.concatenate(
            [jnp.broadcast_to(b_scr[8 * p + 1:8 * p + 2, :], (8, width)) for p in range(chunk // 8)], axis=0)
        hi = jnp.concatenate(
            [jnp.broadcast_to(b_scr[8 * p + 5:8 * p + 6, :], (8, width)) for p in range(chunk // 8)], axis=0)
        return jnp.where((row & 4) == 0, lo, hi)

    def one_chunk(c, carry):
        rows = pl.ds(pl.multiple_of(c * chunk, chunk), chunk)
        z = f_ref[rows, :]
        f = lb + one_m_lb * _sigmoid(z)
        logf = jnp.log(jnp.maximum(f, TINY))
        k = one_m_lb * _sigmoid(-z)
        qraw = q_ref[rows, :]
        q = qraw * _sigmoid(qraw)
        v = i_ref[rows, :]
        v16 = v.astype(BF16)

        b = jnp.dot(tri, logf, precision=HIGHEST, preferred_element_type=F32)
        b_scr[...] = b

        b_last = b_scr[chunk - 1:chunk, :]
        qe = (q * jnp.exp(b)).astype(BF16)
        k_dec = (k * jnp.exp(b_last - b)).astype(BF16)
        st = [st_scr[h] for h in range(A_HEADS)]
        o_inter = [lax.dot_general(qe[:, hs], st[h].astype(BF16), NT_DIMS, preferred_element_type=F32)
                   for h, hs in enumerate(heads)]
        upd = [jnp.dot(v[:, hs].T.astype(BF16), k_dec[:, hs], preferred_element_type=F32) for hs in heads]
        decay_last = jnp.exp(b_last)
        for h, hs in enumerate(heads):
            st_scr[h] = st[h] * decay_last[:, hs] + upd[h]

        qts, kts = [], []
        for m in levels:
            right = (row & m) != 0
            if m == 1:
                arg = jnp.where(right, logf, 0.0)
            else:
                d = b - boundary_rows(m)
                arg = jnp.where(right, d, -d)
            e = jnp.exp(arg)
            qts.append(jnp.where(right, q * e, 0.0).astype(BF16))
            kts.append(jnp.where(right, 0.0, k * e).astype(BF16))
        prods = [[lax.dot_general(qts[lev][:, hs], kts[lev][:, hs], NT_DIMS, preferred_element_type=F32)
                  for hs in heads] for lev in range(len(levels))]
        attn = []
        for h in range(A_HEADS):
            a = mask_ref[0] * prods[0][h]
            for lev in range(1, len(levels)):
                a = a + mask_ref[lev] * prods[lev][h]
            attn.append(a.astype(BF16))
        o_intra = [jnp.dot(attn[h], v16[:, hs], preferred_element_type=F32) for h, hs in enumerate(heads)]

        qk = q * k
        gate = g_ref[rows, :]
        gate = gate * _sigmoid(gate)
        for h, hs in enumerate(heads):
            diag = jnp.sum(qk[:, hs], axis=-1, keepdims=True)
            o = o_intra[h] + o_inter[h] + diag * v[:, hs]
            o_ref[rows, hs] = (_rms(o, onorm) * gate[:, hs]).astype(o_ref.dtype)
        return carry

    lax.fori_loop(0, n_chunks, one_chunk, 0)


def _hgrn(proj3, lb, onorm, *, ts):
    bsz, s, _ = proj3.shape
    ts = min(ts, s)
    chunk = HGRN_CHUNK
    tri = jnp.asarray(np.tril(np.ones((chunk, chunk), np.float32)))
    masks = jnp.asarray(_hgrn_masks(chunk))
    nlev = masks.shape[0]

    def group(cb):
        return pl.BlockSpec((None, ts, A_W), lambda b, i: (b, i, cb // A_HEADS))

    return pl.pallas_call(
        functools.partial(_hgrn_kernel, chunk=chunk, n_chunks=ts // chunk),
        grid=(bsz, s // ts),
        in_specs=[
            group(CB_AQ), group(CB_AF), group(CB_AI), group(CB_AG),
            pl.BlockSpec((1, A_W), lambda b, i: (0, 0)),
            pl.BlockSpec((1, HEAD_DIM), lambda b, i: (0, 0)),
            pl.BlockSpec((chunk, chunk), lambda b, i: (0, 0)),
            pl.BlockSpec((nlev, chunk, chunk), lambda b, i: (0, 0, 0)),
        ],
        out_specs=pl.BlockSpec((None, ts, A_W), lambda b, i: (b, i, 0)),
        out_shape=jax.ShapeDtypeStruct((bsz, s, A_W), BF16),
        scratch_shapes=[pltpu.VMEM((A_HEADS, HEAD_DIM, HEAD_DIM), F32), pltpu.VMEM((chunk, A_W), F32)],
        compiler_params=_params("parallel", "arbitrary"),
        name="hgrn2",
    )(proj3, proj3, proj3, proj3, lb.reshape(1, A_W), onorm.reshape(1, HEAD_DIM), tri, masks)


ATT_BLOCK = 256


def _store_transposed_blocks(o_ref, x):
    for j in range(x.shape[0] // ATT_BLOCK):
        o_ref[j] = x[j * ATT_BLOCK:(j + 1) * ATT_BLOCK].T.astype(o_ref.dtype)


def _fox_prep_kernel(q_ref, k_ref, v_ref, fc_ref, qg_ref, kg_ref, qo_ref, ko_ref, vo_ref, fo_ref):
    _store_transposed_blocks(qo_ref, _rms(q_ref[...], qg_ref[...]) * Q_SCALE)
    ko_ref[...] = _rms(k_ref[...], kg_ref[...]).astype(BF16)
    _store_transposed_blocks(vo_ref, v_ref[...])
    fc = fc_ref[...]
    lane = lax.broadcasted_iota(jnp.int32, fc.shape, 1)
    mine = jnp.sum(jnp.where(lane == pl.program_id(1), fc, 0.0), axis=-1, keepdims=True)
    fo_ref[...] = jnp.broadcast_to(mine, fc.shape)


def _fox_prep(proj3, fcum, qg, kg, *, ts):
    bsz, s, _ = proj3.shape
    ts = min(ts, s)
    nblk = ts // ATT_BLOCK

    def col(cb):
        return pl.BlockSpec((None, ts, HEAD_DIM), lambda b, h, i: (b, i, cb + h))

    gain = pl.BlockSpec((1, HEAD_DIM), lambda b, h, i: (0, 0))
    rows = pl.BlockSpec((None, None, ts, HEAD_DIM), lambda b, h, i: (b, h, i, 0))
    tblk = pl.BlockSpec((None, None, nblk, HEAD_DIM, ATT_BLOCK), lambda b, h, i: (b, h, i, 0, 0))
    rows_shape = (bsz, B_HEADS, s, HEAD_DIM)
    tblk_shape = jax.ShapeDtypeStruct((bsz, B_HEADS, s // ATT_BLOCK, HEAD_DIM, ATT_BLOCK), BF16)
    return pl.pallas_call(
        _fox_prep_kernel,
        grid=(bsz, B_HEADS, s // ts),
        in_specs=[col(CB_BQ), col(CB_BK), col(CB_BV),
                  pl.BlockSpec((None, ts, HEAD_DIM), lambda b, h, i: (b, i, 0)), gain, gain],
        out_specs=[tblk, rows, tblk, rows],
        out_shape=[tblk_shape, jax.ShapeDtypeStruct(rows_shape, BF16), tblk_shape,
                   jax.ShapeDtypeStruct(rows_shape, F32)],
        compiler_params=_params("parallel", "parallel", "parallel"),
        name="fox_prep",
    )(proj3, proj3, proj3, fcum, qg.reshape(1, HEAD_DIM), kg.reshape(1, HEAD_DIM))


def _fox_fcum_kernel(f_ref, bf_ref, tri_ref, o_ref, carry_scr):
    @pl.when(pl.program_id(1) == 0)
    def _():
        carry_scr[...] = jnp.zeros_like(carry_scr)

    x = f_ref[...] + bf_ref[...]
    logf = jnp.minimum(x, 0.0) - jnp.log(1.0 + jnp.exp(-jnp.abs(x)))
    cum = jnp.dot(tri_ref[...], logf, precision=HIGHEST, preferred_element_type=F32) + carry_scr[...]
    o_ref[...] = cum * LOG2E
    carry_scr[...] = cum[cum.shape[0] - 1:, :]


def _fox_fcum(proj3, b_f, *, ts):
    bsz, s, _ = proj3.shape
    ts = min(ts, s)
    tri = jnp.asarray(np.tril(np.ones((ts, ts), np.float32)))
    bf_pad = jnp.zeros((1, HEAD_DIM), F32).at[0, :B_HEADS].set(b_f.astype(F32))
    return pl.pallas_call(
        _fox_fcum_kernel,
        grid=(bsz, s // ts),
        in_specs=[
            pl.BlockSpec((None, ts, HEAD_DIM), lambda b, i: (b, i, CB_BF)),
            pl.BlockSpec((1, HEAD_DIM), lambda b, i: (0, 0)),
            pl.BlockSpec((ts, ts), lambda b, i: (0, 0)),
        ],
        out_specs=pl.BlockSpec((None, ts, HEAD_DIM), lambda b, i: (b, i, 0)),
        out_shape=jax.ShapeDtypeStruct((bsz, s, HEAD_DIM), F32),
        scratch_shapes=[pltpu.VMEM((1, HEAD_DIM), F32)],
        compiler_params=_params("parallel", "arbitrary"),
        name="fox_fcum",
    )(proj3, bf_pad, tri)


def _flash_update(ml, acc_ref, h, s, vt, m_new=None, sub=None):
    m, l = ml
    if m_new is None:
        m_new = jnp.maximum(m, jnp.max(s, axis=0, keepdims=True))
        sub = m_new
    p = jnp.exp2(s - sub)
    alpha = jnp.exp2(m - m_new)
    l = alpha * l + jnp.sum(p, axis=0, keepdims=True)
    acc_ref[h] = alpha * acc_ref[h] + jnp.dot(vt, p.astype(BF16), preferred_element_type=F32)
    return m_new, l


def _flash_init(acc_ref, hp):
    acc_ref[...] = jnp.zeros_like(acc_ref)
    return tuple((jnp.full((1, ATT_BLOCK), NEG, F32), jnp.zeros((1, ATT_BLOCK), F32)) for _ in range(hp))


def _flash_store(o_ref, acc_ref, h, ml):
    o_ref[:, h * HEAD_DIM:(h + 1) * HEAD_DIM] = (acc_ref[h] / ml[1]).T.astype(o_ref.dtype)


def _key_rows(kj):
    return pl.ds(pl.multiple_of(kj * ATT_BLOCK, ATT_BLOCK), ATT_BLOCK)


def _run_blocks(qi, init, put_scores, past_block, own_block, s_even, s_odd):
    def pair(jj, mls):
        kj = 2 * jj
        put_scores(s_odd, kj + 1)
        mls = past_block(kj, mls, s_even)
        put_scores(s_even, kj + 2)
        return past_block(kj + 1, mls, s_odd)

    put_scores(s_even, 0)
    mls = lax.fori_loop(0, qi // 2, pair, init)

    @pl.when(qi % 2 == 0)
    def _():
        own_block(mls, s_even)

    @pl.when(qi % 2 == 1)
    def _():
        put_scores(s_odd, qi)
        own_block(past_block(qi - 1, mls, s_even), s_odd)


def _fox_attn_kernel(q_ref, k_ref, v_ref, fk_ref, o_ref, acc_ref, s_even, s_odd, *, hp):
    qi = pl.program_id(2)

    def put_scores(s_ref, kj):
        for h in range(hp):
            bias = fk_ref[h, _key_rows(kj), :]
            s = jnp.dot(k_ref[h, _key_rows(kj), :], q_ref[h], preferred_element_type=F32)
            s_ref[h] = s - jnp.concatenate([bias] * (ATT_BLOCK // HEAD_DIM), axis=1)

    def past_block(kj, mls, s_ref):
        return tuple(_flash_update(mls[h], acc_ref, h, s_ref[h], v_ref[h, kj]) for h in range(hp))

    def own_block(mls, s_ref):
        kpos = lax.broadcasted_iota(jnp.int32, (ATT_BLOCK, ATT_BLOCK), 0)
        qpos = lax.broadcasted_iota(jnp.int32, (ATT_BLOCK, ATT_BLOCK), 1)
        for h in range(hp):
            s = jnp.where(kpos <= qpos, s_ref[h], NEG)
            _flash_store(o_ref, acc_ref, h, _flash_update(mls[h], acc_ref, h, s, v_ref[h, qi]))

    _run_blocks(qi, _flash_init(acc_ref, hp), put_scores, past_block, own_block, s_even, s_odd)


def _att_specs(s, hp):
    once = pl.Buffered(1)
    nblk = s // ATT_BLOCK
    q_tile = pl.BlockSpec((None, hp, None, HEAD_DIM, ATT_BLOCK), lambda b, h, i: (b, h, i, 0, 0))
    k_full = pl.BlockSpec((None, hp, s, HEAD_DIM), lambda b, h, i: (b, h, 0, 0), pipeline_mode=once)
    v_full = pl.BlockSpec((None, hp, nblk, HEAD_DIM, ATT_BLOCK), lambda b, h, i: (b, h, 0, 0, 0),
                          pipeline_mode=once)
    out = pl.BlockSpec((None, ATT_BLOCK, hp * HEAD_DIM), lambda b, h, i: (b, i, h))
    scores = pltpu.VMEM((hp, ATT_BLOCK, ATT_BLOCK), F32)
    scratch = [pltpu.VMEM((hp, HEAD_DIM, ATT_BLOCK), F32), scores, scores]
    return q_tile, k_full, v_full, out, scratch


def _fox_attn(qt, kn, vt, fkb, *, hp):
    bsz, nh, s, _ = kn.shape
    q_tile, k_full, v_full, out, scratch = _att_specs(s, hp)
    return pl.pallas_call(
        functools.partial(_fox_attn_kernel, hp=hp),
        grid=(bsz, nh // hp, s // ATT_BLOCK),
        in_specs=[q_tile, k_full, v_full, k_full],
        out_specs=out,
        out_shape=jax.ShapeDtypeStruct((bsz, s, nh * HEAD_DIM), BF16),
        scratch_shapes=scratch,
        compiler_params=_params("parallel", "parallel", "arbitrary"),
        name="fox_attn",
    )(qt, kn, vt, fkb)


def _moba_prep_kernel(q_ref, k_ref, v_ref, qg_ref, kg_ref, qo_ref, ko_ref, vo_ref, sel_ref, *, topk):
    qn = _rms(q_ref[...], qg_ref[...])
    kn = _rms(k_ref[...], kg_ref[...])
    _store_transposed_blocks(qo_ref, qn * Q_SCALE)
    ko_ref[...] = kn.astype(BF16)
    _store_transposed_blocks(vo_ref, v_ref[...])

    s = qn.shape[0]
    nb = s // MOBA_BLOCK
    nbp = sel_ref.shape[0]
    kbar = jnp.sum(kn.reshape(nb, MOBA_BLOCK, HEAD_DIM), axis=1) * (1.0 / MOBA_BLOCK)
    if nbp > nb:
        kbar = jnp.concatenate([kbar, jnp.zeros((nbp - nb, HEAD_DIM), F32)], axis=0)
    gate = lax.dot_general(kbar, qn, NT_DIMS, precision=HIGHEST, preferred_element_type=F32)
    blk = lax.broadcasted_iota(jnp.int32, (nbp, s), 0)
    own = lax.broadcasted_iota(jnp.int32, (nbp, s), 1) // MOBA_BLOCK
    gate = jnp.where(blk < own, gate, NEG)
    sel = jnp.zeros((nbp, s), F32)
    for _ in range(topk):
        mx = jnp.max(gate, axis=0, keepdims=True)
        idx = jnp.min(jnp.where(gate == mx, blk, nbp), axis=0, keepdims=True)
        hit = blk == idx
        sel = jnp.where(hit & (mx > NEG / 2), 1.0, sel)
        gate = jnp.where(hit, -jnp.inf, gate)
    sel_ref[...] = sel


def _moba_prep(proj3, qg, kg):
    bsz, s, _ = proj3.shape
    nb = s // MOBA_BLOCK
    nbp = -(-nb // 8) * 8
    topk = min(MOBA_TOPK, nb - 1)

    def col(cb):
        return pl.BlockSpec((None, s, HEAD_DIM), lambda b, h: (b, 0, cb + h))

    gain = pl.BlockSpec((1, HEAD_DIM), lambda b, h: (0, 0))
    rows = pl.BlockSpec((None, None, s, HEAD_DIM), lambda b, h: (b, h, 0, 0))
    tblk = pl.BlockSpec((None, None, s // ATT_BLOCK, HEAD_DIM, ATT_BLOCK), lambda b, h: (b, h, 0, 0, 0))
    tblk_shape = jax.ShapeDtypeStruct((bsz, C_HEADS, s // ATT_BLOCK, HEAD_DIM, ATT_BLOCK), BF16)
    return pl.pallas_call(
        functools.partial(_moba_prep_kernel, topk=topk),
        grid=(bsz, C_HEADS),
        in_specs=[col(CB_CQ), col(CB_CK), col(CB_CV), gain, gain],
        out_specs=[tblk, rows, tblk, pl.BlockSpec((None, None, nbp, s), lambda b, h: (b, h, 0, 0))],
        out_shape=[tblk_shape, jax.ShapeDtypeStruct((bsz, C_HEADS, s, HEAD_DIM), BF16), tblk_shape,
                   jax.ShapeDtypeStruct((bsz, C_HEADS, nbp, s), F32)],
        compiler_params=_params("parallel", "parallel"),
        name="moba_prep",
    )(proj3, proj3, proj3, qg.reshape(1, HEAD_DIM), kg.reshape(1, HEAD_DIM))


def _moba_attn_kernel(q_ref, k_ref, v_ref, sel_ref, slope_ref, o_ref, acc_ref, s_even, s_odd, *, hp):
    assert ATT_BLOCK == MOBA_BLOCK
    qi = pl.program_id(2)
    rel = (lax.broadcasted_iota(jnp.int32, (ATT_BLOCK, ATT_BLOCK), 0)
           - lax.broadcasted_iota(jnp.int32, (ATT_BLOCK, ATT_BLOCK), 1))
    relf = rel.astype(F32)
    slopes = [slope_ref[h, 0:1, 0:1] for h in range(hp)]
    alibi = [slopes[h] * relf for h in range(hp)]

    def put_scores(s_ref, kj):
        for h in range(hp):
            s = jnp.dot(k_ref[h, _key_rows(kj), :], q_ref[h], preferred_element_type=F32)
            s_ref[h] = s + alibi[h]

    def past_block(kj, mls, s_ref):
        dist = ((kj - qi) * ATT_BLOCK).astype(F32)
        out = []
        for h in range(hp):
            s = s_ref[h]
            chosen = sel_ref[h, pl.ds(kj, 1), :] > 0.5
            shift = slopes[h] * dist
            blk_max = jnp.max(s, axis=0, keepdims=True) + shift
            m_new = jnp.maximum(mls[h][0], jnp.where(chosen, blk_max, NEG))
            sub = jnp.where(chosen, m_new - shift, jnp.inf)
            out.append(_flash_update(mls[h], acc_ref, h, s, v_ref[h, kj], m_new=m_new, sub=sub))
        return tuple(out)

    def own_block(mls, s_ref):
        for h in range(hp):
            s = jnp.where(rel <= 0, s_ref[h], NEG)
            _flash_store(o_ref, acc_ref, h, _flash_update(mls[h], acc_ref, h, s, v_ref[h, qi]))

    _run_blocks(qi, _flash_init(acc_ref, hp), put_scores, past_block, own_block, s_even, s_odd)


def _moba_attn(qt, kn, vt, sel, *, hp):
    bsz, nh, s, _ = kn.shape
    nbp = sel.shape[2]
    slopes = np.exp2(-8.0 * np.arange(1, nh + 1, dtype=np.float32) / nh).astype(np.float32) * LOG2E
    slopes = jnp.asarray(np.broadcast_to(slopes[:, None, None], (nh, 8, HEAD_DIM)).copy())
    q_tile, k_full, v_full, out, scratch = _att_specs(s, hp)
    return pl.pallas_call(
        functools.partial(_moba_attn_kernel, hp=hp),
        grid=(bsz, nh // hp, s // ATT_BLOCK),
        in_specs=[q_tile, k_full, v_full,
                  pl.BlockSpec((None, hp, nbp, ATT_BLOCK), lambda b, h, i: (b, h, 0, i)),
                  pl.BlockSpec((hp, 8, HEAD_DIM), lambda b, h, i: (h, 0, 0))],
        out_specs=out,
        out_shape=jax.ShapeDtypeStruct((bsz, s, nh * HEAD_DIM), BF16),
        scratch_shapes=scratch,
        compiler_params=_params("parallel", "parallel", "arbitrary"),
        name="moba_attn",
    )(qt, kn, vt, sel, slopes)


def _out_proj_kernel(h_ref, a_ref, b_ref, c_ref, w_ref, o_ref):
    acc = jnp.dot(a_ref[...], w_ref[:A_W, :], preferred_element_type=F32)
    acc = acc + jnp.dot(b_ref[...], w_ref[A_W:A_W + B_W, :], preferred_element_type=F32)
    acc = acc + jnp.dot(c_ref[...], w_ref[A_W + B_W:, :], preferred_element_type=F32)
    o_ref[...] = h_ref[...] + acc


def _out_proj(h, oa, ob, oc, w_out, layer, *, tm, tn):
    t, d = h.shape
    tm = min(tm, t)

    def act(w):
        return pl.BlockSpec((tm, w), lambda i, j: (i, 0))

    return pl.pallas_call(
        _out_proj_kernel,
        grid=(t // tm, d // tn),
        in_specs=[pl.BlockSpec((tm, tn), lambda i, j: (i, j)), act(A_W), act(B_W), act(C_W),
                  pl.BlockSpec((None, A_W + B_W + C_W, tn), lambda i, j: (layer, 0, j))],
        out_specs=pl.BlockSpec((tm, tn), lambda i, j: (i, j)),
        out_shape=jax.ShapeDtypeStruct((t, d), F32),
        compiler_params=_params("parallel", "arbitrary"),
        name="out_proj",
    )(h, oa, ob, oc, w_out)


HALO = 8


def _ffn_kernel(h_ref, halo_ref, g_ref, wg_ref, wu_ref, cw_ref, cb_ref, wd_ref, o_ref, c_scr, acc_scr,
                *, tiles_per_seq):
    i = pl.program_id(0)
    j = pl.program_id(1)
    tm = h_ref.shape[0]

    @pl.when(j == 0)
    def _():
        g = g_ref[...]
        c_scr[HALO:, :] = _rms(h_ref[...], g).astype(BF16)
        prev = jnp.where(i % tiles_per_seq == 0, 0.0, 1.0) * _rms(halo_ref[...], g)
        c_scr[:HALO, :] = prev.astype(BF16)
        acc_scr[...] = jnp.zeros_like(acc_scr)

    c_all = c_scr[...]
    hg = jnp.dot(c_all, wg_ref[...], preferred_element_type=F32)
    hu = jnp.dot(c_all[HALO:], wu_ref[...], preferred_element_type=F32)
    cw = cw_ref[...]
    conv = (hg[HALO - 2:HALO - 2 + tm] * cw[0:1] + hg[HALO - 1:HALO - 1 + tm] * cw[1:2]
            + hg[HALO:] * cw[2:3] + cb_ref[...])
    gelu = 0.5 * conv * (1.0 + jnp.tanh(0.7978845608028654 * (conv + 0.044715 * conv * conv * conv)))
    acc_scr[...] += jnp.dot((gelu * hu).astype(BF16), wd_ref[...], preferred_element_type=F32)

    @pl.when(j == pl.num_programs(1) - 1)
    def _():
        o_ref[...] = h_ref[...] + acc_scr[...]


def _ffn(h, g, w_gate, w_up, conv_w, conv_b, w_down, layer, *, seq, tm, tf):
    t, d = h.shape
    ff = w_gate.shape[2]
    tm = min(tm, seq)
    cw = jnp.zeros((8, ff), F32).at[:conv_w.shape[0]].set(conv_w.astype(F32))
    return pl.pallas_call(
        functools.partial(_ffn_kernel, tiles_per_seq=seq // tm),
        grid=(t // tm, ff // tf),
        in_specs=[
            pl.BlockSpec((tm, d), lambda i, j: (i, 0)),
            pl.BlockSpec((HALO, d), lambda i, j: (jnp.maximum(i * (tm // HALO) - 1, 0), 0)),
            pl.BlockSpec((1, d), lambda i, j: (0, 0)),
            pl.BlockSpec((None, d, tf), lambda i, j: (layer, 0, j)),
            pl.BlockSpec((None, d, tf), lambda i, j: (layer, 0, j)),
            pl.BlockSpec((8, tf), lambda i, j: (0, j)),
            pl.BlockSpec((1, tf), lambda i, j: (0, j)),
            pl.BlockSpec((None, tf, d), lambda i, j: (layer, j, 0)),
        ],
        out_specs=pl.BlockSpec((tm, d), lambda i, j: (i, 0)),
        out_shape=jax.ShapeDtypeStruct((t, d), F32),
        scratch_shapes=[pltpu.VMEM((HALO + tm, d), BF16), pltpu.VMEM((tm, d), F32)],
        compiler_params=_params("parallel", "arbitrary"),
        name="conv_ffn",
    )(h, h, g.reshape(1, d), w_gate, w_up, cw, conv_b.reshape(1, ff).astype(F32), w_down)


def _ple_kernel(h_ref, hcol_ref, g_ref, p_ref, wg_ref, wp_ref, o_ref, e_scr, p_scr):
    @pl.when(pl.program_id(1) == 0)
    def _():
        e_scr[...] = _rms(h_ref[...], g_ref[...]).astype(BF16)
        p_scr[...] = p_ref[...].astype(BF16)

    gate = _sigmoid(jnp.dot(e_scr[...], wg_ref[...], preferred_element_type=F32))
    emb = jnp.dot(p_scr[...], wp_ref[...], preferred_element_type=F32)
    o_ref[...] = hcol_ref[...] + gate * emb


def _ple(h, g, p, w_gate, w_proj, layer, *, tm, tn):
    t, d = h.shape
    pd = p.shape[2]
    tm = min(tm, t)
    return pl.pallas_call(
        _ple_kernel,
        grid=(t // tm, d // tn),
        in_specs=[
            pl.BlockSpec((tm, d), lambda i, j: (i, 0)),
            pl.BlockSpec((tm, tn), lambda i, j: (i, j)),
            pl.BlockSpec((1, d), lambda i, j: (0, 0)),
            pl.BlockSpec((None, tm, pd), lambda i, j: (layer, i, 0)),
            pl.BlockSpec((None, d, tn), lambda i, j: (layer, 0, j)),
            pl.BlockSpec((None, pd, tn), lambda i, j: (layer, 0, j)),
        ],
        out_specs=pl.BlockSpec((tm, tn), lambda i, j: (i, j)),
        out_shape=jax.ShapeDtypeStruct((t, d), F32),
        scratch_shapes=[pltpu.VMEM((tm, d), BF16), pltpu.VMEM((tm, pd), BF16)],
        compiler_params=_params("parallel", "arbitrary"),
        name="ple",
    )(h, h, g.reshape(1, d), p, w_gate, w_proj)


def _pad_w_in(w):
    lead = w.shape[:-1]
    cut = CB_BF * HEAD_DIM
    w = w.astype(BF16)
    zeros = functools.partial(jnp.zeros, dtype=BF16)
    return jnp.concatenate(
        [w[..., :cut], w[..., cut:cut + B_HEADS], zeros(lead + (HEAD_DIM - B_HEADS,)), w[..., cut + B_HEADS:],
         zeros(lead + (HEAD_DIM,))], axis=-1)


def _mixers(proj3, lb, hgrn_onorm, fox_bf, fox_qn, fox_kn, moba_qn, moba_kn):
    bsz, s, _ = proj3.shape
    oa = _hgrn(proj3, lb, hgrn_onorm, ts=512)

    fcum = _fox_fcum(proj3, fox_bf, ts=256)
    bq, bk, bv, fkb = _fox_prep(proj3, fcum, fox_qn, fox_kn, ts=1024)
    ob = _fox_attn(bq, bk, bv, fkb, hp=FOX_HEADS_PER_STEP)

    cq, ck, cv, sel = _moba_prep(proj3, moba_qn, moba_kn)
    oc = _moba_attn(cq, ck, cv, sel, hp=MOBA_HEADS_PER_STEP)
    return oa, ob, oc


def kernel(x, p, attn_norm, w_in, fox_bf, lb_logits, hgrn_onorm, fox_qnorm, fox_knorm, moba_qnorm,
           moba_knorm, w_out, ffn_norm, w_gate, w_up, conv_w, conv_b, w_down, ple_norm, w_ple_gate,
           w_ple_proj):
    bsz, s, d = x.shape
    depth = w_in.shape[0]
    t = bsz * s
    lbs = _lower_bounds(lb_logits)
    h = x.reshape(t, d).astype(F32)
    w_in = _pad_w_in(w_in)
    w_out, w_gate, w_up, w_down, w_ple_gate, w_ple_proj = (
        w.astype(BF16) for w in (w_out, w_gate, w_up, w_down, w_ple_gate, w_ple_proj))
    p = p.reshape(depth, t, -1)
    for i in range(depth):
        proj = _norm_matmul(h, attn_norm[i], w_in, i, tm=1024, tn=768)
        proj3 = proj.reshape(bsz, s, IN_COLS_PADDED)
        oa, ob, oc = _mixers(proj3, lbs[i], hgrn_onorm[i], fox_bf[i], fox_qnorm[i], fox_knorm[i],
                             moba_qnorm[i], moba_knorm[i])
        h = _out_proj(h, oa.reshape(t, A_W), ob.reshape(t, B_W), oc.reshape(t, C_W), w_out, i,
                      tm=1024, tn=512)
        h = _ffn(h, ffn_norm[i], w_gate, w_up, conv_w[i], conv_b[i], w_down, i, seq=s, tm=512, tf=512)
        h = _ple(h, ple_norm[i], p, w_ple_gate, w_ple_proj, i, tm=1024, tn=512)
    return h.reshape(bsz, s, d).astype(x.dtype)
```

```python
import functools

import numpy as np
import jax
import jax.numpy as jnp
from jax import lax
from jax.experimental import pallas as pl
from jax.experimental.pallas import tpu as pltpu

F32 = jnp.float32
BF16 = jnp.bfloat16
HIGHEST = lax.Precision.HIGHEST

HEAD_DIM = 128
A_HEADS = 4
B_HEADS = 6
C_HEADS = 6
A_W = A_HEADS * HEAD_DIM
B_W = B_HEADS * HEAD_DIM
C_W = C_HEADS * HEAD_DIM
MOBA_BLOCK = 256
MOBA_TOPK = 3
EPS = 1e-6
TINY = 1e-30
NEG = -1e30
LOG2E = 1.4426950408889634
Q_SCALE = HEAD_DIM ** -0.5 * LOG2E

CB_AQ, CB_AF, CB_AI, CB_AG = 0, 4, 8, 12
CB_BQ, CB_BK, CB_BV, CB_BF = 16, 22, 28, 34
CB_CQ, CB_CK, CB_CV = 35, 41, 47
IN_COLS_PADDED = 54 * HEAD_DIM

HGRN_CHUNK = 64
FOX_HEADS_PER_STEP = 6
MOBA_HEADS_PER_STEP = 6
VMEM_LIMIT = 56 * 1024 * 1024

NT_DIMS = (((1,), (1,)), ((), ()))


def _params(*sem):
    return pltpu.CompilerParams(dimension_semantics=sem, vmem_limit_bytes=VMEM_LIMIT)


def _sigmoid(x):
    return 1.0 / (1.0 + jnp.exp(-x))


def _rms(x, g):
    ms = jnp.mean(x * x, axis=-1, keepdims=True)
    return (x * lax.rsqrt(ms + EPS)) * g


def _lower_bounds_kernel(x_ref, o_ref):
    x = x_ref[...]
    depth = x.shape[0]
    m = x[0:1]
    for i in range(1, depth):
        m = jnp.maximum(m, x[i:i + 1])
    e = jnp.exp(x - m)
    tot = e[0:1]
    for i in range(1, depth):
        tot = tot + e[i:i + 1]
    sm = e / tot
    run = jnp.zeros_like(m)
    for i in range(depth):
        run = run + sm[i:i + 1]
        o_ref[i:i + 1, :] = run - sm[0:1]


def _lower_bounds(lb_logits):
    return pl.pallas_call(
        _lower_bounds_kernel,
        out_shape=jax.ShapeDtypeStruct(lb_logits.shape, F32),
    )(lb_logits.astype(F32))


def _norm_matmul_kernel(x_ref, g_ref, w_ref, o_ref, a_scr):
    @pl.when(pl.program_id(1) == 0)
    def _():
        a_scr[...] = _rms(x_ref[...], g_ref[...]).astype(BF16)

    o_ref[...] = jnp.dot(a_scr[...], w_ref[...], preferred_element_type=F32).astype(o_ref.dtype)


def _norm_matmul(x, g, w, layer, *, tm, tn):
    t, k = x.shape
    n = w.shape[2]
    tm = min(tm, t)
    return pl.pallas_call(
        _norm_matmul_kernel,
        grid=(t // tm, n // tn),
        in_specs=[
            pl.BlockSpec((tm, k), lambda i, j: (i, 0)),
            pl.BlockSpec((1, k), lambda i, j: (0, 0)),
            pl.BlockSpec((None, k, tn), lambda i, j: (layer, 0, j)),
        ],
        out_specs=pl.BlockSpec((tm, tn), lambda i, j: (i, j)),
        out_shape=jax.ShapeDtypeStruct((t, n), F32),
        scratch_shapes=[pltpu.VMEM((tm, k), BF16)],
        compiler_params=_params("parallel", "arbitrary"),
        name="norm_in_proj",
    )(x, g.reshape(1, k), w)


def _hgrn_levels(chunk):
    levels = []
    m = chunk // 2
    while m >= 1:
        levels.append(m)
        m //= 2
    return levels


def _hgrn_masks(chunk):
    t = np.arange(chunk)[:, None]
    s = np.arange(chunk)[None, :]
    out = []
    for m in _hgrn_levels(chunk):
        same_parent = (t // (2 * m)) == (s // (2 * m))
        out.append((same_parent & ((t // m) % 2 == 1) & ((s // m) % 2 == 0)).astype(np.float32))
    return np.stack(out)


def _hgrn_kernel(q_ref, f_ref, i_ref, g_ref, lb_ref, on_ref, tri_ref, mask_ref, o_ref, st_scr, b_scr,
                 *, chunk, n_chunks):
    @pl.when(pl.program_id(1) == 0)
    def _():
        st_scr[...] = jnp.zeros_like(st_scr)

    width = A_W
    lb = lb_ref[...]
    one_m_lb = 1.0 - lb
    onorm = on_ref[...]
    tri = tri_ref[...]
    row = lax.broadcasted_iota(jnp.int32, (chunk, width), 0)
    levels = _hgrn_levels(chunk)
    heads = [slice(h * HEAD_DIM, (h + 1) * HEAD_DIM) for h in range(A_HEADS)]

    def boundary_rows(m):
        if 2 * m >= 8:
            pieces = [
                jnp.broadcast_to(b_scr[2 * m * p + m - 1:2 * m * p + m, :], (2 * m, width))
                for p in range(chunk // (2 * m))
            ]
            return pieces[0] if len(pieces) == 1 else jnp.concatenate(pieces, axis=0)
        assert m == 2
        lo = jnp.concatenate(
            [jnp.broadcast_to(b_scr[8 * p + 1:8 * p + 2, :], (8, width)) for p in range(chunk // 8)], axis=0)
        hi = jnp.concatenate(
            [jnp.broadcast_to(b_scr[8 * p + 5:8 * p + 6, :], (8, width)) for p in range(chunk // 8)], axis=0)
        return jnp.where((row & 4) == 0, lo, hi)

    def one_chunk(c, carry):
        rows = pl.ds(pl.multiple_of(c * chunk, chunk), chunk)
        z = f_ref[rows, :]
        f = lb + one_m_lb * _sigmoid(z)
        logf = jnp.log(jnp.maximum(f, TINY))
        k = one_m_lb * _sigmoid(-z)
        qraw = q_ref[rows, :]
        q = qraw * _sigmoid(qraw)
        v = i_ref[rows, :]
        v16 = v.astype(BF16)

        b = jnp.dot(tri, logf, precision=HIGHEST, preferred_element_type=F32)
        b_scr[...] = b

        b_last = b_scr[chunk - 1:chunk, :]
        qe = (q * jnp.exp(b)).astype(BF16)
        k_dec = (k * jnp.exp(b_last - b)).astype(BF16)
        st = [st_scr[h] for h in range(A_HEADS)]
        o_inter = [lax.dot_general(qe[:, hs], st[h].astype(BF16), NT_DIMS, preferred_element_type=F32)
                   for h, hs in enumerate(heads)]
        upd = [jnp.dot(v[:, hs].T.astype(BF16), k_dec[:, hs], preferred_element_type=F32) for hs in heads]
        decay_last = jnp.exp(b_last)
        for h, hs in enumerate(heads):
            st_scr[h] = st[h] * decay_last[:, hs] + upd[h]

        qts, kts = [], []
        for m in levels:
            right = (row & m) != 0
            if m == 1:
                arg = jnp.where(right, logf, 0.0)
            else:
                d = b - boundary_rows(m)
                arg = jnp.where(right, d, -d)
            e = jnp.exp(arg)
            qts.append(jnp.where(right, q * e, 0.0).astype(BF16))
            kts.append(jnp.where(right, 0.0, k * e).astype(BF16))
        prods = [[lax.dot_general(qts[lev][:, hs], kts[lev][:, hs], NT_DIMS, preferred_element_type=F32)
                  for hs in heads] for lev in range(len(levels))]
        attn = []
        for h in range(A_HEADS):
            a = mask_ref[0] * prods[0][h]
            for lev in range(1, len(levels)):
                a = a + mask_ref[lev] * prods[lev][h]
            attn.append(a.astype(BF16))
        o_intra = [jnp.dot(attn[h], v16[:, hs], preferred_element_type=F32) for h, hs in enumerate(heads)]

        qk = q * k
        gate = g_ref[rows, :]
        gate = gate * _sigmoid(gate)
        for h, hs in enumerate(heads):
            diag = jnp.sum(qk[:, hs], axis=-1, keepdims=True)
            o = o_intra[h] + o_inter[h] + diag * v[:, hs]
            o_ref[rows, hs] = (_rms(o, onorm) * gate[:, hs]).astype(o_ref.dtype)
        return carry

    lax.fori_loop(0, n_chunks, one_chunk, 0)


def _hgrn(proj3, lb, onorm, *, ts):
    bsz, s, _ = proj3.shape
    ts = min(ts, s)
    chunk = HGRN_CHUNK
    tri = jnp.asarray(np.tril(np.ones((chunk, chunk), np.float32)))
    masks = jnp.asarray(_hgrn_masks(chunk))
    nlev = masks.shape[0]

    def group(cb):
        return pl.BlockSpec((None, ts, A_W), lambda b, i: (b, i, cb // A_HEADS))

    return pl.pallas_call(
        functools.partial(_hgrn_kernel, chunk=chunk, n_chunks=ts // chunk),
        grid=(bsz, s // ts),
        in_specs=[
            group(CB_AQ), group(CB_AF), group(CB_AI), group(CB_AG),
            pl.BlockSpec((1, A_W), lambda b, i: (0, 0)),
            pl.BlockSpec((1, HEAD_DIM), lambda b, i: (0, 0)),
            pl.BlockSpec((chunk, chunk), lambda b, i: (0, 0)),
            pl.BlockSpec((nlev, chunk, chunk), lambda b, i: (0, 0, 0)),
        ],
        out_specs=pl.BlockSpec((None, ts, A_W), lambda b, i: (b, i, 0)),
        out_shape=jax.ShapeDtypeStruct((bsz, s, A_W), BF16),
        scratch_shapes=[pltpu.VMEM((A_HEADS, HEAD_DIM, HEAD_DIM), F32), pltpu.VMEM((chunk, A_W), F32)],
        compiler_params=_params("parallel", "arbitrary"),
        name="hgrn2",
    )(proj3, proj3, proj3, proj3, lb.reshape(1, A_W), onorm.reshape(1, HEAD_DIM), tri, masks)


ATT_BLOCK = 256
ATT_K = 2 * HEAD_DIM
V_ROWS = HEAD_DIM + 16
N_BIAS_PIECES = 3


def _bias_columns(bias):
    lane = lax.broadcasted_iota(jnp.int32, (bias.shape[0], HEAD_DIM), 1)
    out = jnp.zeros((bias.shape[0], HEAD_DIM), F32)
    rest = bias
    for i in range(N_BIAS_PIECES):
        piece = rest.astype(BF16).astype(F32)
        out = jnp.where(lane == i, piece, out)
        rest = rest - piece
    return out


def _store_q_blocks(o_ref, q, bias_sign):
    row = lax.broadcasted_iota(jnp.int32, (ATT_K - HEAD_DIM, ATT_BLOCK), 0)
    tail = jnp.where(row < N_BIAS_PIECES, bias_sign, 0.0).astype(F32)
    for j in range(q.shape[0] // ATT_BLOCK):
        qt = q[j * ATT_BLOCK:(j + 1) * ATT_BLOCK].T
        o_ref[j] = jnp.concatenate([qt, tail], axis=0).astype(o_ref.dtype)


def _store_v_blocks(o_ref, v):
    row = lax.broadcasted_iota(jnp.int32, (V_ROWS - HEAD_DIM, ATT_BLOCK), 0)
    tail = jnp.where(row == 0, 1.0, 0.0).astype(F32)
    for j in range(v.shape[0] // ATT_BLOCK):
        vt = v[j * ATT_BLOCK:(j + 1) * ATT_BLOCK].T
        o_ref[j] = jnp.concatenate([vt, tail], axis=0).astype(o_ref.dtype)


def _fox_prep_kernel(q_ref, k_ref, v_ref, fc_ref, qg_ref, kg_ref, qo_ref, ko_ref, vo_ref):
    _store_q_blocks(qo_ref, _rms(q_ref[...], qg_ref[...]) * Q_SCALE, -1.0)
    fc = fc_ref[...]
    lane = lax.broadcasted_iota(jnp.int32, fc.shape, 1)
    mine = jnp.sum(jnp.where(lane == pl.program_id(1), fc, 0.0), axis=-1, keepdims=True)
    kn = _rms(k_ref[...], kg_ref[...])
    ko_ref[...] = jnp.concatenate([kn, _bias_columns(mine)], axis=1).astype(BF16)
    _store_v_blocks(vo_ref, v_ref[...])


def _att_prep_shapes(bsz, nh, s):
    nblk = s // ATT_BLOCK
    return [jax.ShapeDtypeStruct((bsz, nh, nblk, ATT_K, ATT_BLOCK), BF16),
            jax.ShapeDtypeStruct((bsz, nh, s, ATT_K), BF16),
            jax.ShapeDtypeStruct((bsz, nh, nblk, V_ROWS, ATT_BLOCK), BF16)]


def _fox_prep(proj3, fcum, qg, kg, *, ts):
    bsz, s, _ = proj3.shape
    ts = min(ts, s)
    nblk = ts // ATT_BLOCK

    def col(cb):
        return pl.BlockSpec((None, ts, HEAD_DIM), lambda b, h, i: (b, i, cb + h))

    gain = pl.BlockSpec((1, HEAD_DIM), lambda b, h, i: (0, 0))
    return pl.pallas_call(
        _fox_prep_kernel,
        grid=(bsz, B_HEADS, s // ts),
        in_specs=[col(CB_BQ), col(CB_BK), col(CB_BV),
                  pl.BlockSpec((None, ts, HEAD_DIM), lambda b, h, i: (b, i, 0)), gain, gain],
        out_specs=[pl.BlockSpec((None, None, nblk, ATT_K, ATT_BLOCK), lambda b, h, i: (b, h, i, 0, 0)),
                   pl.BlockSpec((None, None, ts, ATT_K), lambda b, h, i: (b, h, i, 0)),
                   pl.BlockSpec((None, None, nblk, V_ROWS, ATT_BLOCK), lambda b, h, i: (b, h, i, 0, 0))],
        out_shape=_att_prep_shapes(bsz, B_HEADS, s),
        compiler_params=_params("parallel", "parallel", "parallel"),
        name="fox_prep",
    )(proj3, proj3, proj3, fcum, qg.reshape(1, HEAD_DIM), kg.reshape(1, HEAD_DIM))


def _fox_fcum_kernel(f_ref, bf_ref, tri_ref, o_ref, carry_scr):
    @pl.when(pl.program_id(1) == 0)
    def _():
        carry_scr[...] = jnp.zeros_like(carry_scr)

    x = f_ref[...] + bf_ref[...]
    logf = jnp.minimum(x, 0.0) - jnp.log(1.0 + jnp.exp(-jnp.abs(x)))
    cum = jnp.dot(tri_ref[...], logf, precision=HIGHEST, preferred_element_type=F32) + carry_scr[...]
    o_ref[...] = cum * LOG2E
    carry_scr[...] = cum[cum.shape[0] - 1:, :]


def _fox_fcum(proj3, b_f, *, ts):
    bsz, s, _ = proj3.shape
    ts = min(ts, s)
    tri = jnp.asarray(np.tril(np.ones((ts, ts), np.float32)))
    bf_pad = jnp.zeros((1, HEAD_DIM), F32).at[0, :B_HEADS].set(b_f.astype(F32))
    return pl.pallas_call(
        _fox_fcum_kernel,
        grid=(bsz, s // ts),
        in_specs=[
            pl.BlockSpec((None, ts, HEAD_DIM), lambda b, i: (b, i, CB_BF)),
            pl.BlockSpec((1, HEAD_DIM), lambda b, i: (0, 0)),
            pl.BlockSpec((ts, ts), lambda b, i: (0, 0)),
        ],
        out_specs=pl.BlockSpec((None, ts, HEAD_DIM), lambda b, i: (b, i, 0)),
        out_shape=jax.ShapeDtypeStruct((bsz, s, HEAD_DIM), F32),
        scratch_shapes=[pltpu.VMEM((1, HEAD_DIM), F32)],
        compiler_params=_params("parallel", "arbitrary"),
        name="fox_fcum",
    )(proj3, bf_pad, tri)


def _flash_update(m, acc_ref, h, s, vt, m_new=None, sub=None):
    if m_new is None:
        m_new = jnp.maximum(m, jnp.max(s, axis=0, keepdims=True))
        sub = m_new
    p = jnp.exp2(s - sub)
    alpha = jnp.exp2(m - m_new)
    acc_ref[h] = alpha * acc_ref[h] + jnp.dot(vt, p.astype(BF16), preferred_element_type=F32)
    return m_new


def _flash_init(acc_ref, hp):
    acc_ref[...] = jnp.zeros_like(acc_ref)
    return tuple(jnp.full((1, ATT_BLOCK), NEG, F32) for _ in range(hp))


def _flash_store(o_ref, acc_ref, h):
    acc = acc_ref[h]
    out = acc[:HEAD_DIM] / acc[HEAD_DIM:HEAD_DIM + 1]
    o_ref[:, h * HEAD_DIM:(h + 1) * HEAD_DIM] = out.T.astype(o_ref.dtype)


def _key_rows(kj):
    return pl.ds(pl.multiple_of(kj * ATT_BLOCK, ATT_BLOCK), ATT_BLOCK)


def _run_blocks(qi, init, put_scores, past_block, own_block, s_even, s_odd):
    def pair(jj, ms):
        kj = 2 * jj
        put_scores(s_odd, kj + 1)
        ms = past_block(kj, ms, s_even)
        put_scores(s_even, kj + 2)
        return past_block(kj + 1, ms, s_odd)

    put_scores(s_even, 0)
    ms = lax.fori_loop(0, qi // 2, pair, init)

    @pl.when(qi % 2 == 0)
    def _():
        own_block(ms, s_even)

    @pl.when(qi % 2 == 1)
    def _():
        put_scores(s_odd, qi)
        own_block(past_block(qi - 1, ms, s_even), s_odd)


def _attn_kernel(q_ref, k_ref, v_ref, *rest, hp, selective):
    if selective:
        sel_ref, o_ref, acc_ref, s_even, s_odd = rest
    else:
        o_ref, acc_ref, s_even, s_odd = rest
    qi = pl.program_id(2)

    def put_scores(s_ref, kj):
        for h in range(hp):
            s_ref[h] = jnp.dot(k_ref[h, _key_rows(kj), :], q_ref[h], preferred_element_type=F32)

    def past_block(kj, ms, s_ref):
        out = []
        for h in range(hp):
            s = s_ref[h]
            if selective:
                chosen = sel_ref[h, pl.ds(kj, 1), :] > 0.5
                blk_max = jnp.max(s, axis=0, keepdims=True)
                m_new = jnp.maximum(ms[h], jnp.where(chosen, blk_max, NEG))
                sub = jnp.where(chosen, m_new, jnp.inf)
                out.append(_flash_update(ms[h], acc_ref, h, s, v_ref[h, kj], m_new=m_new, sub=sub))
            else:
                out.append(_flash_update(ms[h], acc_ref, h, s, v_ref[h, kj]))
        return tuple(out)

    def own_block(ms, s_ref):
        kpos = lax.broadcasted_iota(jnp.int32, (ATT_BLOCK, ATT_BLOCK), 0)
        qpos = lax.broadcasted_iota(jnp.int32, (ATT_BLOCK, ATT_BLOCK), 1)
        for h in range(hp):
            s = jnp.where(kpos <= qpos, s_ref[h], NEG)
            _flash_update(ms[h], acc_ref, h, s, v_ref[h, qi])
            _flash_store(o_ref, acc_ref, h)

    _run_blocks(qi, _flash_init(acc_ref, hp), put_scores, past_block, own_block, s_even, s_odd)


def _attn(qt, kn, vt, sel, *, hp, name):
    bsz, nh, s, _ = kn.shape
    nblk = s // ATT_BLOCK
    once = pl.Buffered(1)
    in_specs = [
        pl.BlockSpec((None, hp, None, ATT_K, ATT_BLOCK), lambda b, h, i: (b, h, i, 0, 0)),
        pl.BlockSpec((None, hp, s, ATT_K), lambda b, h, i: (b, h, 0, 0), pipeline_mode=once),
        pl.BlockSpec((None, hp, nblk, V_ROWS, ATT_BLOCK), lambda b, h, i: (b, h, 0, 0, 0), pipeline_mode=once),
    ]
    args = [qt, kn, vt]
    if sel is not None:
        in_specs.append(pl.BlockSpec((None, hp, sel.shape[2], ATT_BLOCK), lambda b, h, i: (b, h, 0, i)))
        args.append(sel)
    scores = pltpu.VMEM((hp, ATT_BLOCK, ATT_BLOCK), F32)
    return pl.pallas_call(
        functools.partial(_attn_kernel, hp=hp, selective=sel is not None),
        grid=(bsz, nh // hp, s // ATT_BLOCK),
        in_specs=in_specs,
        out_specs=pl.BlockSpec((None, ATT_BLOCK, hp * HEAD_DIM), lambda b, h, i: (b, i, h)),
        out_shape=jax.ShapeDtypeStruct((bsz, s, nh * HEAD_DIM), BF16),
        scratch_shapes=[pltpu.VMEM((hp, V_ROWS, ATT_BLOCK), F32), scores, scores],
        compiler_params=_params("parallel", "parallel", "arbitrary"),
        name=name,
    )(*args)


def _moba_prep_kernel(q_ref, k_ref, v_ref, qg_ref, kg_ref, bias_ref, qo_ref, ko_ref, vo_ref, sel_ref, *, topk):
    qn = _rms(q_ref[...], qg_ref[...])
    kn = _rms(k_ref[...], kg_ref[...])
    s = qn.shape[0]
    _store_q_blocks(qo_ref, qn * Q_SCALE, 1.0)
    ko_ref[...] = jnp.concatenate([kn.astype(BF16), bias_ref[...]], axis=1)
    _store_v_blocks(vo_ref, v_ref[...])

    nb = s // MOBA_BLOCK
    nbp = sel_ref.shape[0]
    kbar = jnp.sum(kn.reshape(nb, MOBA_BLOCK, HEAD_DIM), axis=1) * (1.0 / MOBA_BLOCK)
    if nbp > nb:
        kbar = jnp.concatenate([kbar, jnp.zeros((nbp - nb, HEAD_DIM), F32)], axis=0)
    gate = lax.dot_general(kbar, qn, NT_DIMS, precision=HIGHEST, preferred_element_type=F32)
    blk = lax.broadcasted_iota(jnp.int32, (nbp, s), 0)
    own = lax.broadcasted_iota(jnp.int32, (nbp, s), 1) // MOBA_BLOCK
    gate = jnp.where(blk < own, gate, NEG)
    sel = jnp.zeros((nbp, s), F32)
    for _ in range(topk):
        mx = jnp.max(gate, axis=0, keepdims=True)
        idx = jnp.min(jnp.where(gate == mx, blk, nbp), axis=0, keepdims=True)
        hit = blk == idx
        sel = jnp.where(hit & (mx > NEG / 2), 1.0, sel)
        gate = jnp.where(hit, -jnp.inf, gate)
    sel_ref[...] = sel


def _alibi_key_bias(s):
    slopes = np.exp2(-8.0 * np.arange(1, C_HEADS + 1, dtype=np.float32) / C_HEADS).astype(np.float32)
    slopes = slopes * np.float32(LOG2E)
    rest = slopes[:, None] * np.arange(s, dtype=np.float32)[None, :]
    out = np.zeros((C_HEADS, s, HEAD_DIM), np.float32)
    for i in range(N_BIAS_PIECES):
        piece = rest.astype(BF16).astype(np.float32)
        out[:, :, i] = piece
        rest = rest - piece
    return out.astype(BF16)


def _moba_prep(proj3, qg, kg):
    assert ATT_BLOCK == MOBA_BLOCK
    bsz, s, _ = proj3.shape
    nb = s // MOBA_BLOCK
    nbp = -(-nb // 8) * 8
    topk = min(MOBA_TOPK, nb - 1)
    bias = jnp.asarray(_alibi_key_bias(s))

    def col(cb):
        return pl.BlockSpec((None, s, HEAD_DIM), lambda b, h: (b, 0, cb + h))

    gain = pl.BlockSpec((1, HEAD_DIM), lambda b, h: (0, 0))
    nblk = s // ATT_BLOCK
    return pl.pallas_call(
        functools.partial(_moba_prep_kernel, topk=topk),
        grid=(bsz, C_HEADS),
        in_specs=[col(CB_CQ), col(CB_CK), col(CB_CV), gain, gain,
                  pl.BlockSpec((None, s, HEAD_DIM), lambda b, h: (h, 0, 0))],
        out_specs=[pl.BlockSpec((None, None, nblk, ATT_K, ATT_BLOCK), lambda b, h: (b, h, 0, 0, 0)),
                   pl.BlockSpec((None, None, s, ATT_K), lambda b, h: (b, h, 0, 0)),
                   pl.BlockSpec((None, None, nblk, V_ROWS, ATT_BLOCK), lambda b, h: (b, h, 0, 0, 0)),
                   pl.BlockSpec((None, None, nbp, s), lambda b, h: (b, h, 0, 0))],
        out_shape=_att_prep_shapes(bsz, C_HEADS, s) + [jax.ShapeDtypeStruct((bsz, C_HEADS, nbp, s), F32)],
        compiler_params=_params("parallel", "parallel"),
        name="moba_prep",
    )(proj3, proj3, proj3, qg.reshape(1, HEAD_DIM), kg.reshape(1, HEAD_DIM), bias)


def _out_proj_kernel(h_ref, a_ref, b_ref, c_ref, w_ref, o_ref):
    acc = jnp.dot(a_ref[...], w_ref[:A_W, :], preferred_element_type=F32)
    acc = acc + jnp.dot(b_ref[...], w_ref[A_W:A_W + B_W, :], preferred_element_type=F32)
    acc = acc + jnp.dot(c_ref[...], w_ref[A_W + B_W:, :], preferred_element_type=F32)
    o_ref[...] = h_ref[...] + acc


def _out_proj(h, oa, ob, oc, w_out, layer, *, tm, tn):
    t, d = h.shape
    tm = min(tm, t)

    def act(w):
        return pl.BlockSpec((tm, w), lambda i, j: (i, 0))

    return pl.pallas_call(
        _out_proj_kernel,
        grid=(t // tm, d // tn),
        in_specs=[pl.BlockSpec((tm, tn), lambda i, j: (i, j)), act(A_W), act(B_W), act(C_W),
                  pl.BlockSpec((None, A_W + B_W + C_W, tn), lambda i, j: (layer, 0, j))],
        out_specs=pl.BlockSpec((tm, tn), lambda i, j: (i, j)),
        out_shape=jax.ShapeDtypeStruct((t, d), F32),
        compiler_params=_params("parallel", "arbitrary"),
        name="out_proj",
    )(h, oa, ob, oc, w_out)


HALO = 16


def _ffn_kernel(h_ref, halo_ref, g_ref, wg_ref, wu_ref, cw_ref, cb_ref, wd_ref, o_ref, c_scr, acc_scr,
                *, tiles_per_seq):
    i = pl.program_id(0)
    j = pl.program_id(1)
    tm = h_ref.shape[0]

    @pl.when(j == 0)
    def _():
        g = g_ref[...]
        c_scr[HALO:, :] = _rms(h_ref[...], g).astype(BF16)
        prev = jnp.where(i % tiles_per_seq == 0, 0.0, 1.0) * _rms(halo_ref[...], g)
        c_scr[:HALO, :] = prev.astype(BF16)
        acc_scr[...] = jnp.zeros_like(acc_scr)

    c_all = c_scr[...]
    hg = jnp.dot(c_all, wg_ref[...], preferred_element_type=F32)
    hu = jnp.dot(c_all[HALO:], wu_ref[...], preferred_element_type=F32)
    cw = cw_ref[...]
    conv = (hg[HALO - 2:HALO - 2 + tm] * cw[0:1] + hg[HALO - 1:HALO - 1 + tm] * cw[1:2]
            + hg[HALO:] * cw[2:3] + cb_ref[...])
    gelu = 0.5 * conv * (1.0 + jnp.tanh(0.7978845608028654 * (conv + 0.044715 * conv * conv * conv)))
    acc_scr[...] += jnp.dot((gelu * hu).astype(BF16), wd_ref[...], preferred_element_type=F32)

    @pl.when(j == pl.num_programs(1) - 1)
    def _():
        o_ref[...] = h_ref[...] + acc_scr[...]


def _ffn(h, g, w_gate, w_up, conv_w, conv_b, w_down, layer, *, seq, tm, tf):
    t, d = h.shape
    ff = w_gate.shape[2]
    tm = min(tm, seq)
    cw = jnp.zeros((8, ff), F32).at[:conv_w.shape[0]].set(conv_w.astype(F32))
    return pl.pallas_call(
        functools.partial(_ffn_kernel, tiles_per_seq=seq // tm),
        grid=(t // tm, ff // tf),
        in_specs=[
            pl.BlockSpec((tm, d), lambda i, j: (i, 0)),
            pl.BlockSpec((HALO, d), lambda i, j: (jnp.maximum(i * (tm // HALO) - 1, 0), 0)),
            pl.BlockSpec((1, d), lambda i, j: (0, 0)),
            pl.BlockSpec((None, d, tf), lambda i, j: (layer, 0, j)),
            pl.BlockSpec((None, d, tf), lambda i, j: (layer, 0, j)),
            pl.BlockSpec((8, tf), lambda i, j: (0, j)),
            pl.BlockSpec((1, tf), lambda i, j: (0, j)),
            pl.BlockSpec((None, tf, d), lambda i, j: (layer, j, 0)),
        ],
        out_specs=pl.BlockSpec((tm, d), lambda i, j: (i, 0)),
        out_shape=jax.ShapeDtypeStruct((t, d), F32),
        scratch_shapes=[pltpu.VMEM((HALO + tm, d), BF16), pltpu.VMEM((tm, d), F32)],
        compiler_params=_params("parallel", "arbitrary"),
        name="conv_ffn",
    )(h, h, g.reshape(1, d), w_gate, w_up, cw, conv_b.reshape(1, ff).astype(F32), w_down)


def _ple_kernel(h_ref, hcol_ref, g_ref, p_ref, wg_ref, wp_ref, o_ref, e_scr, p_scr):
    @pl.when(pl.program_id(1) == 0)
    def _():
        e_scr[...] = _rms(h_ref[...], g_ref[...]).astype(BF16)
        p_scr[...] = p_ref[...].astype(BF16)

    gate = _sigmoid(jnp.dot(e_scr[...], wg_ref[...], preferred_element_type=F32))
    emb = jnp.dot(p_scr[...], wp_ref[...], preferred_element_type=F32)
    o_ref[...] = hcol_ref[...] + gate * emb


def _ple(h, g, p, w_gate, w_proj, layer, *, tm, tn):
    t, d = h.shape
    pd = p.shape[2]
    tm = min(tm, t)
    return pl.pallas_call(
        _ple_kernel,
        grid=(t // tm, d // tn),
        in_specs=[
            pl.BlockSpec((tm, d), lambda i, j: (i, 0)),
            pl.BlockSpec((tm, tn), lambda i, j: (i, j)),
            pl.BlockSpec((1, d), lambda i, j: (0, 0)),
            pl.BlockSpec((None, tm, pd), lambda i, j: (layer, i, 0)),
            pl.BlockSpec((None, d, tn), lambda i, j: (layer, 0, j)),
            pl.BlockSpec((None, pd, tn), lambda i, j: (layer, 0, j)),
        ],
        out_specs=pl.BlockSpec((tm, tn), lambda i, j: (i, j)),
        out_shape=jax.ShapeDtypeStruct((t, d), F32),
        scratch_shapes=[pltpu.VMEM((tm, d), BF16), pltpu.VMEM((tm, pd), BF16)],
        compiler_params=_params("parallel", "arbitrary"),
        name="ple",
    )(h, h, g.reshape(1, d), p, w_gate, w_proj)


def _pad_w_in(w):
    lead = w.shape[:-1]
    cut = CB_BF * HEAD_DIM
    w = w.astype(BF16)
    zeros = functools.partial(jnp.zeros, dtype=BF16)
    return jnp.concatenate(
        [w[..., :cut], w[..., cut:cut + B_HEADS], zeros(lead + (HEAD_DIM - B_HEADS,)), w[..., cut + B_HEADS:],
         zeros(lead + (HEAD_DIM,))], axis=-1)


def _mixers(proj3, lb, hgrn_onorm, fox_bf, fox_qn, fox_kn, moba_qn, moba_kn):
    bsz, s, _ = proj3.shape
    oa = _hgrn(proj3, lb, hgrn_onorm, ts=512)

    fcum = _fox_fcum(proj3, fox_bf, ts=256)
    bq, bk, bv = _fox_prep(proj3, fcum, fox_qn, fox_kn, ts=1024)
    ob = _attn(bq, bk, bv, None, hp=FOX_HEADS_PER_STEP, name="fox_attn")

    cq, ck, cv, sel = _moba_prep(proj3, moba_qn, moba_kn)
    oc = _attn(cq, ck, cv, sel, hp=MOBA_HEADS_PER_STEP, name="moba_attn")
    return oa, ob, oc


def kernel(x, p, attn_norm, w_in, fox_bf, lb_logits, hgrn_onorm, fox_qnorm, fox_knorm, moba_qnorm,
           moba_knorm, w_out, ffn_norm, w_gate, w_up, conv_w, conv_b, w_down, ple_norm, w_ple_gate,
           w_ple_proj):
    bsz, s, d = x.shape
    depth = w_in.shape[0]
    t = bsz * s
    lbs = _lower_bounds(lb_logits)
    h = x.reshape(t, d).astype(F32)
    w_in = _pad_w_in(w_in)
    w_out, w_gate, w_up, w_down, w_ple_gate, w_ple_proj = (
        w.astype(BF16) for w in (w_out, w_gate, w_up, w_down, w_ple_gate, w_ple_proj))
    p = p.reshape(depth, t, -1)
    for i in range(depth):
        proj = _norm_matmul(h, attn_norm[i], w_in, i, tm=1024, tn=768)
        proj3 = proj.reshape(bsz, s, IN_COLS_PADDED)
        oa, ob, oc = _mixers(proj3, lbs[i], hgrn_onorm[i], fox_bf[i], fox_qnorm[i], fox_knorm[i],
                             moba_qnorm[i], moba_knorm[i])
        h = _out_proj(h, oa.reshape(t, A_W), ob.reshape(t, B_W), oc.reshape(t, C_W), w_out, i,
                      tm=1024, tn=512)
        h = _ffn(h, ffn_norm[i], w_gate, w_up, conv_w[i], conv_b[i], w_down, i, seq=s, tm=512, tf=512)
        h = _ple(h, ple_norm[i], p, w_ple_gate, w_ple_proj, i, tm=1024, tn=512)
    return h.reshape(bsz, s, d).astype(x.dtype)
```

```python
import functools

import numpy as np
import jax
import jax.numpy as jnp
from jax import lax
from jax.experimental import pallas as pl
from jax.experimental.pallas import tpu as pltpu

F32 = jnp.float32
BF16 = jnp.bfloat16
HIGHEST = lax.Precision.HIGHEST

HEAD_DIM = 128
A_HEADS = 4
B_HEADS = 6
C_HEADS = 6
A_W = A_HEADS * HEAD_DIM
B_W = B_HEADS * HEAD_DIM
C_W = C_HEADS * HEAD_DIM
MOBA_BLOCK = 256
MOBA_TOPK = 3
EPS = 1e-6
TINY = 1e-30
NEG = -1e30
LOG2E = 1.4426950408889634
Q_SCALE = HEAD_DIM ** -0.5 * LOG2E

CB_AQ, CB_AF, CB_AI, CB_AG = 0, 4, 8, 12
CB_BQ, CB_BK, CB_BV, CB_BF = 16, 22, 28, 34
CB_CQ, CB_CK, CB_CV = 35, 41, 47
IN_COLS_PADDED = 54 * HEAD_DIM

HGRN_CHUNK = 128
FOX_HEADS_PER_STEP = 6
MOBA_HEADS_PER_STEP = 6
VMEM_LIMIT = 56 * 1024 * 1024

NT_DIMS = (((1,), (1,)), ((), ()))


def _params(*sem):
    return pltpu.CompilerParams(dimension_semantics=sem, vmem_limit_bytes=VMEM_LIMIT)


def _sigmoid(x):
    return 1.0 / (1.0 + jnp.exp(-x))


def _rms(x, g):
    ms = jnp.mean(x * x, axis=-1, keepdims=True)
    return (x * lax.rsqrt(ms + EPS)) * g


def _lower_bounds_kernel(x_ref, o_ref):
    x = x_ref[...]
    depth = x.shape[0]
    m = x[0:1]
    for i in range(1, depth):
        m = jnp.maximum(m, x[i:i + 1])
    e = jnp.exp(x - m)
    tot = e[0:1]
    for i in range(1, depth):
        tot = tot + e[i:i + 1]
    sm = e / tot
    run = jnp.zeros_like(m)
    for i in range(depth):
        run = run + sm[i:i + 1]
        o_ref[i:i + 1, :] = run - sm[0:1]


def _lower_bounds(lb_logits):
    return pl.pallas_call(
        _lower_bounds_kernel,
        out_shape=jax.ShapeDtypeStruct(lb_logits.shape, F32),
    )(lb_logits.astype(F32))


def _norm_matmul_kernel(x_ref, g_ref, w_ref, o_ref, a_scr):
    @pl.when(pl.program_id(1) == 0)
    def _():
        a_scr[...] = _rms(x_ref[...], g_ref[...]).astype(BF16)

    o_ref[...] = jnp.dot(a_scr[...], w_ref[...], preferred_element_type=F32).astype(o_ref.dtype)


def _norm_matmul(x, g, w, layer, *, tm, tn):
    t, k = x.shape
    n = w.shape[2]
    tm = min(tm, t)
    return pl.pallas_call(
        _norm_matmul_kernel,
        grid=(t // tm, n // tn),
        in_specs=[
            pl.BlockSpec((tm, k), lambda i, j: (i, 0)),
            pl.BlockSpec((1, k), lambda i, j: (0, 0)),
            pl.BlockSpec((None, k, tn), lambda i, j: (layer, 0, j)),
        ],
        out_specs=pl.BlockSpec((tm, tn), lambda i, j: (i, j)),
        out_shape=jax.ShapeDtypeStruct((t, n), F32),
        scratch_shapes=[pltpu.VMEM((tm, k), BF16)],
        compiler_params=_params("parallel", "arbitrary"),
        name="norm_in_proj",
    )(x, g.reshape(1, k), w)


def _hgrn_levels(chunk):
    levels = []
    m = chunk // 2
    while m >= 1:
        levels.append(m)
        m //= 2
    return levels


def _hgrn_masks(chunk):
    t = np.arange(chunk)[:, None]
    s = np.arange(chunk)[None, :]
    out = []
    for m in _hgrn_levels(chunk):
        same_parent = (t // (2 * m)) == (s // (2 * m))
        out.append((same_parent & ((t // m) % 2 == 1) & ((s // m) % 2 == 0)).astype(np.float32))
    return np.stack(out)


def _hgrn_kernel(q_ref, f_ref, i_ref, g_ref, lb_ref, on_ref, tri_ref, mask_ref, o_ref, st_scr, b_scr,
                 *, chunk, n_chunks):
    @pl.when(pl.program_id(1) == 0)
    def _():
        st_scr[...] = jnp.zeros_like(st_scr)

    width = A_W
    lb = lb_ref[...]
    one_m_lb = 1.0 - lb
    onorm = on_ref[...]
    tri = tri_ref[...]
    row = lax.broadcasted_iota(jnp.int32, (chunk, width), 0)
    levels = _hgrn_levels(chunk)
    heads = [slice(h * HEAD_DIM, (h + 1) * HEAD_DIM) for h in range(A_HEADS)]

    def boundary_rows(m):
        if 2 * m >= 8:
            pieces = [
                jnp.broadcast_to(b_scr[2 * m * p + m - 1:2 * m * p + m, :], (2 * m, width))
                for p in range(chunk // (2 * m))
            ]
            return pieces[0] if len(pieces) == 1 else jnp.concatenate(pieces, axis=0)
        assert m == 2
        lo = jnp.concatenate(
            [jnp.broadcast_to(b_scr[8 * p + 1:8 * p + 2, :], (8, width)) for p in range(chunk // 8)], axis=0)
        hi = jnp.concatenate(
            [jnp.broadcast_to(b_scr[8 * p + 5:8 * p + 6, :], (8, width)) for p in range(chunk // 8)], axis=0)
        return jnp.where((row & 4) == 0, lo, hi)

    def one_chunk(c, carry):
        rows = pl.ds(pl.multiple_of(c * chunk, chunk), chunk)
        z = f_ref[rows, :]
        f = lb + one_m_lb * _sigmoid(z)
        logf = jnp.log2(jnp.maximum(f, TINY))
        k = one_m_lb * _sigmoid(-z)
        qraw = q_ref[rows, :]
        q = qraw * _sigmoid(qraw)
        v = i_ref[rows, :]
        v16 = v.astype(BF16)

        b = jnp.dot(tri, logf, precision=HIGHEST, preferred_element_type=F32)
        b_scr[...] = b

        b_last = b_scr[chunk - 1:chunk, :]
        qe = (q * jnp.exp2(b)).astype(BF16)
        k_dec = (k * jnp.exp2(b_last - b)).astype(BF16)
        st = [st_scr[h] for h in range(A_HEADS)]
        o_inter = [lax.dot_general(qe[:, hs], st[h].astype(BF16), NT_DIMS, preferred_element_type=F32)
                   for h, hs in enumerate(heads)]
        upd = [jnp.dot(v[:, hs].T.astype(BF16), k_dec[:, hs], preferred_element_type=F32) for hs in heads]
        decay_last = jnp.exp2(b_last)
        for h, hs in enumerate(heads):
            st_scr[h] = st[h] * decay_last[:, hs] + upd[h]

        qts, kts = [], []
        for m in levels:
            right = (row & m) != 0
            if m == 1:
                arg = jnp.where(right, logf, 0.0)
            else:
                d = b - boundary_rows(m)
                arg = jnp.where(right, d, -d)
            scaled = jnp.where(right, q, k) * jnp.exp2(arg)
            qts.append(jnp.where(right, scaled, 0.0).astype(BF16))
            kts.append(jnp.where(right, 0.0, scaled).astype(BF16))
        prods = [[lax.dot_general(qts[lev][:, hs], kts[lev][:, hs], NT_DIMS, preferred_element_type=F32)
                  for hs in heads] for lev in range(len(levels))]
        attn = []
        for h in range(A_HEADS):
            a = mask_ref[0] * prods[0][h]
            for lev in range(1, len(levels)):
                a = a + mask_ref[lev] * prods[lev][h]
            attn.append(a.astype(BF16))
        o_intra = [jnp.dot(attn[h], v16[:, hs], preferred_element_type=F32) for h, hs in enumerate(heads)]

        qk = q * k
        gate = g_ref[rows, :]
        gate = gate * _sigmoid(gate)
        for h, hs in enumerate(heads):
            diag = jnp.sum(qk[:, hs], axis=-1, keepdims=True)
            o = o_intra[h] + o_inter[h] + diag * v[:, hs]
            o_ref[rows, hs] = (_rms(o, onorm) * gate[:, hs]).astype(o_ref.dtype)
        return carry

    lax.fori_loop(0, n_chunks, one_chunk, 0)


def _hgrn(proj3, lb, onorm, *, ts):
    bsz, s, _ = proj3.shape
    ts = min(ts, s)
    chunk = HGRN_CHUNK
    tri = jnp.asarray(np.tril(np.ones((chunk, chunk), np.float32)))
    masks = jnp.asarray(_hgrn_masks(chunk))
    nlev = masks.shape[0]

    def group(cb):
        return pl.BlockSpec((None, ts, A_W), lambda b, i: (b, i, cb // A_HEADS))

    return pl.pallas_call(
        functools.partial(_hgrn_kernel, chunk=chunk, n_chunks=ts // chunk),
        grid=(bsz, s // ts),
        in_specs=[
            group(CB_AQ), group(CB_AF), group(CB_AI), group(CB_AG),
            pl.BlockSpec((1, A_W), lambda b, i: (0, 0)),
            pl.BlockSpec((1, HEAD_DIM), lambda b, i: (0, 0)),
            pl.BlockSpec((chunk, chunk), lambda b, i: (0, 0)),
            pl.BlockSpec((nlev, chunk, chunk), lambda b, i: (0, 0, 0)),
        ],
        out_specs=pl.BlockSpec((None, ts, A_W), lambda b, i: (b, i, 0)),
        out_shape=jax.ShapeDtypeStruct((bsz, s, A_W), BF16),
        scratch_shapes=[pltpu.VMEM((A_HEADS, HEAD_DIM, HEAD_DIM), F32), pltpu.VMEM((chunk, A_W), F32)],
        compiler_params=_params("parallel", "arbitrary"),
        name="hgrn2",
    )(proj3, proj3, proj3, proj3, lb.reshape(1, A_W), onorm.reshape(1, HEAD_DIM), tri, masks)


ATT_BLOCK = 256
ATT_K = 2 * HEAD_DIM
V_ROWS = HEAD_DIM + 16
N_BIAS_PIECES = 3


def _bias_columns(bias):
    lane = lax.broadcasted_iota(jnp.int32, (bias.shape[0], HEAD_DIM), 1)
    out = jnp.zeros((bias.shape[0], HEAD_DIM), F32)
    rest = bias
    for i in range(N_BIAS_PIECES):
        piece = rest.astype(BF16).astype(F32)
        out = jnp.where(lane == i, piece, out)
        rest = rest - piece
    return out


def _store_q_blocks(o_ref, q, bias_sign):
    row = lax.broadcasted_iota(jnp.int32, (ATT_K - HEAD_DIM, ATT_BLOCK), 0)
    tail = jnp.where(row < N_BIAS_PIECES, bias_sign, 0.0).astype(F32)
    for j in range(q.shape[0] // ATT_BLOCK):
        qt = q[j * ATT_BLOCK:(j + 1) * ATT_BLOCK].T
        o_ref[j] = jnp.concatenate([qt, tail], axis=0).astype(o_ref.dtype)


def _store_v_blocks(o_ref, v):
    row = lax.broadcasted_iota(jnp.int32, (V_ROWS - HEAD_DIM, ATT_BLOCK), 0)
    tail = jnp.where(row == 0, 1.0, 0.0).astype(F32)
    for j in range(v.shape[0] // ATT_BLOCK):
        vt = v[j * ATT_BLOCK:(j + 1) * ATT_BLOCK].T
        o_ref[j] = jnp.concatenate([vt, tail], axis=0).astype(o_ref.dtype)


def _fox_prep_kernel(q_ref, k_ref, v_ref, fc_ref, qg_ref, kg_ref, qo_ref, ko_ref, vo_ref):
    _store_q_blocks(qo_ref, _rms(q_ref[...], qg_ref[...]) * Q_SCALE, -1.0)
    fc = fc_ref[...]
    lane = lax.broadcasted_iota(jnp.int32, fc.shape, 1)
    mine = jnp.sum(jnp.where(lane == pl.program_id(1), fc, 0.0), axis=-1, keepdims=True)
    kn = _rms(k_ref[...], kg_ref[...])
    ko_ref[...] = jnp.concatenate([kn, _bias_columns(mine)], axis=1).astype(BF16)
    _store_v_blocks(vo_ref, v_ref[...])


def _att_prep_shapes(bsz, nh, s):
    nblk = s // ATT_BLOCK
    return [jax.ShapeDtypeStruct((bsz, nh, nblk, ATT_K, ATT_BLOCK), BF16),
            jax.ShapeDtypeStruct((bsz, nh, s, ATT_K), BF16),
            jax.ShapeDtypeStruct((bsz, nh, nblk, V_ROWS, ATT_BLOCK), BF16)]


def _fox_prep(proj3, fcum, qg, kg, *, ts):
    bsz, s, _ = proj3.shape
    ts = min(ts, s)
    nblk = ts // ATT_BLOCK

    def col(cb):
        return pl.BlockSpec((None, ts, HEAD_DIM), lambda b, h, i: (b, i, cb + h))

    gain = pl.BlockSpec((1, HEAD_DIM), lambda b, h, i: (0, 0))
    return pl.pallas_call(
        _fox_prep_kernel,
        grid=(bsz, B_HEADS, s // ts),
        in_specs=[col(CB_BQ), col(CB_BK), col(CB_BV),
                  pl.BlockSpec((None, ts, HEAD_DIM), lambda b, h, i: (b, i, 0)), gain, gain],
        out_specs=[pl.BlockSpec((None, None, nblk, ATT_K, ATT_BLOCK), lambda b, h, i: (b, h, i, 0, 0)),
                   pl.BlockSpec((None, None, ts, ATT_K), lambda b, h, i: (b, h, i, 0)),
                   pl.BlockSpec((None, None, nblk, V_ROWS, ATT_BLOCK), lambda b, h, i: (b, h, i, 0, 0))],
        out_shape=_att_prep_shapes(bsz, B_HEADS, s),
        compiler_params=_params("parallel", "parallel", "parallel"),
        name="fox_prep",
    )(proj3, proj3, proj3, fcum, qg.reshape(1, HEAD_DIM), kg.reshape(1, HEAD_DIM))


def _fox_fcum_kernel(f_ref, bf_ref, tri_ref, o_ref, carry_scr):
    @pl.when(pl.program_id(1) == 0)
    def _():
        carry_scr[...] = jnp.zeros_like(carry_scr)

    x = f_ref[...] + bf_ref[...]
    logf = jnp.minimum(x, 0.0) - jnp.log(1.0 + jnp.exp(-jnp.abs(x)))
    cum = jnp.dot(tri_ref[...], logf, precision=HIGHEST, preferred_element_type=F32) + carry_scr[...]
    o_ref[...] = cum * LOG2E
    carry_scr[...] = cum[cum.shape[0] - 1:, :]


def _fox_fcum(proj3, b_f, *, ts):
    bsz, s, _ = proj3.shape
    ts = min(ts, s)
    tri = jnp.asarray(np.tril(np.ones((ts, ts), np.float32)))
    bf_pad = jnp.zeros((1, HEAD_DIM), F32).at[0, :B_HEADS].set(b_f.astype(F32))
    return pl.pallas_call(
        _fox_fcum_kernel,
        grid=(bsz, s // ts),
        in_specs=[
            pl.BlockSpec((None, ts, HEAD_DIM), lambda b, i: (b, i, CB_BF)),
            pl.BlockSpec((1, HEAD_DIM), lambda b, i: (0, 0)),
            pl.BlockSpec((ts, ts), lambda b, i: (0, 0)),
        ],
        out_specs=pl.BlockSpec((None, ts, HEAD_DIM), lambda b, i: (b, i, 0)),
        out_shape=jax.ShapeDtypeStruct((bsz, s, HEAD_DIM), F32),
        scratch_shapes=[pltpu.VMEM((1, HEAD_DIM), F32)],
        compiler_params=_params("parallel", "arbitrary"),
        name="fox_fcum",
    )(proj3, bf_pad, tri)


def _flash_update(m, acc_ref, h, s, vt, m_new=None, sub=None):
    if m_new is None:
        m_new = jnp.maximum(m, jnp.max(s, axis=0, keepdims=True))
        sub = m_new
    p = jnp.exp2(s - sub)
    alpha = jnp.exp2(m - m_new)
    acc_ref[h] = alpha * acc_ref[h] + jnp.dot(vt, p.astype(BF16), preferred_element_type=F32)
    return m_new


def _flash_init(acc_ref, hp):
    acc_ref[...] = jnp.zeros_like(acc_ref)
    return tuple(jnp.full((1, ATT_BLOCK), NEG, F32) for _ in range(hp))


def _flash_store(o_ref, acc_ref, h):
    acc = acc_ref[h]
    out = acc[:HEAD_DIM] / acc[HEAD_DIM:HEAD_DIM + 1]
    o_ref[:, h * HEAD_DIM:(h + 1) * HEAD_DIM] = out.T.astype(o_ref.dtype)


def _key_rows(kj):
    return pl.ds(pl.multiple_of(kj * ATT_BLOCK, ATT_BLOCK), ATT_BLOCK)


def _run_blocks(qi, init, put_scores, past_block, own_block, s_even, s_odd):
    def pair(jj, ms):
        kj = 2 * jj
        put_scores(s_odd, kj + 1)
        ms = past_block(kj, ms, s_even)
        put_scores(s_even, kj + 2)
        return past_block(kj + 1, ms, s_odd)

    put_scores(s_even, 0)
    ms = lax.fori_loop(0, qi // 2, pair, init)

    @pl.when(qi % 2 == 0)
    def _():
        own_block(ms, s_even)

    @pl.when(qi % 2 == 1)
    def _():
        put_scores(s_odd, qi)
        own_block(past_block(qi - 1, ms, s_even), s_odd)


def _attn_kernel(q_ref, k_ref, v_ref, *rest, hp, selective):
    if selective:
        sel_ref, o_ref, acc_ref, s_even, s_odd = rest
    else:
        o_ref, acc_ref, s_even, s_odd = rest
    qi = pl.program_id(2)

    def put_scores(s_ref, kj):
        for h in range(hp):
            s_ref[h] = jnp.dot(k_ref[h, _key_rows(kj), :], q_ref[h], preferred_element_type=F32)

    def past_block(kj, ms, s_ref):
        out = []
        for h in range(hp):
            s = s_ref[h]
            if selective:
                chosen = sel_ref[h, pl.ds(kj, 1), :] > 0.5
                blk_max = jnp.max(s, axis=0, keepdims=True)
                m_new = jnp.maximum(ms[h], jnp.where(chosen, blk_max, NEG))
                sub = jnp.where(chosen, m_new, jnp.inf)
                out.append(_flash_update(ms[h], acc_ref, h, s, v_ref[h, kj], m_new=m_new, sub=sub))
            else:
                out.append(_flash_update(ms[h], acc_ref, h, s, v_ref[h, kj]))
        return tuple(out)

    def own_block(ms, s_ref):
        kpos = lax.broadcasted_iota(jnp.int32, (ATT_BLOCK, ATT_BLOCK), 0)
        qpos = lax.broadcasted_iota(jnp.int32, (ATT_BLOCK, ATT_BLOCK), 1)
        for h in range(hp):
            s = jnp.where(kpos <= qpos, s_ref[h], NEG)
            _flash_update(ms[h], acc_ref, h, s, v_ref[h, qi])
            _flash_store(o_ref, acc_ref, h)

    _run_blocks(qi, _flash_init(acc_ref, hp), put_scores, past_block, own_block, s_even, s_odd)


def _attn(qt, kn, vt, sel, *, hp, name):
    bsz, nh, s, _ = kn.shape
    nblk = s // ATT_BLOCK
    once = pl.Buffered(1)
    in_specs = [
        pl.BlockSpec((None, hp, None, ATT_K, ATT_BLOCK), lambda b, h, i: (b, h, i, 0, 0)),
        pl.BlockSpec((None, hp, s, ATT_K), lambda b, h, i: (b, h, 0, 0), pipeline_mode=once),
        pl.BlockSpec((None, hp, nblk, V_ROWS, ATT_BLOCK), lambda b, h, i: (b, h, 0, 0, 0), pipeline_mode=once),
    ]
    args = [qt, kn, vt]
    if sel is not None:
        in_specs.append(pl.BlockSpec((None, hp, sel.shape[2], ATT_BLOCK), lambda b, h, i: (b, h, 0, i)))
        args.append(sel)
    scores = pltpu.VMEM((hp, ATT_BLOCK, ATT_BLOCK), F32)
    return pl.pallas_call(
        functools.partial(_attn_kernel, hp=hp, selective=sel is not None),
        grid=(bsz, nh // hp, s // ATT_BLOCK),
        in_specs=in_specs,
        out_specs=pl.BlockSpec((None, ATT_BLOCK, hp * HEAD_DIM), lambda b, h, i: (b, i, h)),
        out_shape=jax.ShapeDtypeStruct((bsz, s, nh * HEAD_DIM), BF16),
        scratch_shapes=[pltpu.VMEM((hp, V_ROWS, ATT_BLOCK), F32), scores, scores],
        compiler_params=_params("parallel", "parallel", "arbitrary"),
        name=name,
    )(*args)


def _moba_prep_kernel(q_ref, k_ref, v_ref, qg_ref, kg_ref, bias_ref, qo_ref, ko_ref, vo_ref, sel_ref, *, topk):
    qn = _rms(q_ref[...], qg_ref[...])
    kn = _rms(k_ref[...], kg_ref[...])
    s = qn.shape[0]
    _store_q_blocks(qo_ref, qn * Q_SCALE, 1.0)
    ko_ref[...] = jnp.concatenate([kn.astype(BF16), bias_ref[...]], axis=1)
    _store_v_blocks(vo_ref, v_ref[...])

    nb = s // MOBA_BLOCK
    nbp = sel_ref.shape[0]
    kbar = jnp.sum(kn.reshape(nb, MOBA_BLOCK, HEAD_DIM), axis=1) * (1.0 / MOBA_BLOCK)
    if nbp > nb:
        kbar = jnp.concatenate([kbar, jnp.zeros((nbp - nb, HEAD_DIM), F32)], axis=0)
    gate = lax.dot_general(kbar, qn, NT_DIMS, precision=HIGHEST, preferred_element_type=F32)
    blk = lax.broadcasted_iota(jnp.int32, (nbp, s), 0)
    own = lax.broadcasted_iota(jnp.int32, (nbp, s), 1) // MOBA_BLOCK
    gate = jnp.where(blk < own, gate, NEG)
    sel = jnp.zeros((nbp, s), F32)
    for _ in range(topk):
        mx = jnp.max(gate, axis=0, keepdims=True)
        idx = jnp.min(jnp.where(gate == mx, blk, nbp), axis=0, keepdims=True)
        hit = blk == idx
        sel = jnp.where(hit & (mx > NEG / 2), 1.0, sel)
        gate = jnp.where(hit, -jnp.inf, gate)
    sel_ref[...] = sel


def _alibi_key_bias(s):
    slopes = np.exp2(-8.0 * np.arange(1, C_HEADS + 1, dtype=np.float32) / C_HEADS).astype(np.float32)
    slopes = slopes * np.float32(LOG2E)
    rest = slopes[:, None] * np.arange(s, dtype=np.float32)[None, :]
    out = np.zeros((C_HEADS, s, HEAD_DIM), np.float32)
    for i in range(N_BIAS_PIECES):
        piece = rest.astype(BF16).astype(np.float32)
        out[:, :, i] = piece
        rest = rest - piece
    return out.astype(BF16)


def _moba_prep(proj3, qg, kg):
    assert ATT_BLOCK == MOBA_BLOCK
    bsz, s, _ = proj3.shape
    nb = s // MOBA_BLOCK
    nbp = -(-nb // 8) * 8
    topk = min(MOBA_TOPK, nb - 1)
    bias = jnp.asarray(_alibi_key_bias(s))

    def col(cb):
        return pl.BlockSpec((None, s, HEAD_DIM), lambda b, h: (b, 0, cb + h))

    gain = pl.BlockSpec((1, HEAD_DIM), lambda b, h: (0, 0))
    nblk = s // ATT_BLOCK
    return pl.pallas_call(
        functools.partial(_moba_prep_kernel, topk=topk),
        grid=(bsz, C_HEADS),
        in_specs=[col(CB_CQ), col(CB_CK), col(CB_CV), gain, gain,
                  pl.BlockSpec((None, s, HEAD_DIM), lambda b, h: (h, 0, 0))],
        out_specs=[pl.BlockSpec((None, None, nblk, ATT_K, ATT_BLOCK), lambda b, h: (b, h, 0, 0, 0)),
                   pl.BlockSpec((None, None, s, ATT_K), lambda b, h: (b, h, 0, 0)),
                   pl.BlockSpec((None, None, nblk, V_ROWS, ATT_BLOCK), lambda b, h: (b, h, 0, 0, 0)),
                   pl.BlockSpec((None, None, nbp, s), lambda b, h: (b, h, 0, 0))],
        out_shape=_att_prep_shapes(bsz, C_HEADS, s) + [jax.ShapeDtypeStruct((bsz, C_HEADS, nbp, s), F32)],
        compiler_params=_params("parallel", "parallel"),
        name="moba_prep",
    )(proj3, proj3, proj3, qg.reshape(1, HEAD_DIM), kg.reshape(1, HEAD_DIM), bias)


def _out_proj_kernel(h_ref, a_ref, b_ref, c_ref, w_ref, o_ref):
    acc = jnp.dot(a_ref[...], w_ref[:A_W, :], preferred_element_type=F32)
    acc = acc + jnp.dot(b_ref[...], w_ref[A_W:A_W + B_W, :], preferred_element_type=F32)
    acc = acc + jnp.dot(c_ref[...], w_ref[A_W + B_W:, :], preferred_element_type=F32)
    o_ref[...] = h_ref[...] + acc


def _out_proj(h, oa, ob, oc, w_out, layer, *, tm, tn):
    t, d = h.shape
    tm = min(tm, t)

    def act(w):
        return pl.BlockSpec((tm, w), lambda i, j: (i, 0))

    return pl.pallas_call(
        _out_proj_kernel,
        grid=(t // tm, d // tn),
        in_specs=[pl.BlockSpec((tm, tn), lambda i, j: (i, j)), act(A_W), act(B_W), act(C_W),
                  pl.BlockSpec((None, A_W + B_W + C_W, tn), lambda i, j: (layer, 0, j))],
        out_specs=pl.BlockSpec((tm, tn), lambda i, j: (i, j)),
        out_shape=jax.ShapeDtypeStruct((t, d), F32),
        compiler_params=_params("parallel", "arbitrary"),
        name="out_proj",
    )(h, oa, ob, oc, w_out)


HALO = 16


def _ffn_kernel(h_ref, halo_ref, g_ref, wg_ref, wu_ref, cw_ref, cb_ref, wd_ref, o_ref, c_scr, acc_scr,
                *, tiles_per_seq):
    i = pl.program_id(0)
    j = pl.program_id(1)
    tm = h_ref.shape[0]

    @pl.when(j == 0)
    def _():
        g = g_ref[...]
        c_scr[HALO:, :] = _rms(h_ref[...], g).astype(BF16)
        prev = jnp.where(i % tiles_per_seq == 0, 0.0, 1.0) * _rms(halo_ref[...], g)
        c_scr[:HALO, :] = prev.astype(BF16)
        acc_scr[...] = jnp.zeros_like(acc_scr)

    c_all = c_scr[...]
    hg = jnp.dot(c_all, wg_ref[...], preferred_element_type=F32)
    hu = jnp.dot(c_all[HALO:], wu_ref[...], preferred_element_type=F32)
    cw = cw_ref[...]
    conv = (hg[HALO - 2:HALO - 2 + tm] * cw[0:1] + hg[HALO - 1:HALO - 1 + tm] * cw[1:2]
            + hg[HALO:] * cw[2:3] + cb_ref[...])
    gelu = 0.5 * conv * (1.0 + jnp.tanh(0.7978845608028654 * (conv + 0.044715 * conv * conv * conv)))
    acc_scr[...] += jnp.dot((gelu * hu).astype(BF16), wd_ref[...], preferred_element_type=F32)

    @pl.when(j == pl.num_programs(1) - 1)
    def _():
        o_ref[...] = h_ref[...] + acc_scr[...]


def _ffn(h, g, w_gate, w_up, conv_w, conv_b, w_down, layer, *, seq, tm, tf):
    t, d = h.shape
    ff = w_gate.shape[2]
    tm = min(tm, seq)
    cw = jnp.zeros((8, ff), F32).at[:conv_w.shape[0]].set(conv_w.astype(F32))
    return pl.pallas_call(
        functools.partial(_ffn_kernel, tiles_per_seq=seq // tm),
        grid=(t // tm, ff // tf),
        in_specs=[
            pl.BlockSpec((tm, d), lambda i, j: (i, 0)),
            pl.BlockSpec((HALO, d), lambda i, j: (jnp.maximum(i * (tm // HALO) - 1, 0), 0)),
            pl.BlockSpec((1, d), lambda i, j: (0, 0)),
            pl.BlockSpec((None, d, tf), lambda i, j: (layer, 0, j)),
            pl.BlockSpec((None, d, tf), lambda i, j: (layer, 0, j)),
            pl.BlockSpec((8, tf), lambda i, j: (0, j)),
            pl.BlockSpec((1, tf), lambda i, j: (0, j)),
            pl.BlockSpec((None, tf, d), lambda i, j: (layer, j, 0)),
        ],
        out_specs=pl.BlockSpec((tm, d), lambda i, j: (i, 0)),
        out_shape=jax.ShapeDtypeStruct((t, d), F32),
        scratch_shapes=[pltpu.VMEM((HALO + tm, d), BF16), pltpu.VMEM((tm, d), F32)],
        compiler_params=_params("parallel", "arbitrary"),
        name="conv_ffn",
    )(h, h, g.reshape(1, d), w_gate, w_up, cw, conv_b.reshape(1, ff).astype(F32), w_down)


def _ple_kernel(h_ref, hcol_ref, g_ref, p_ref, wg_ref, wp_ref, o_ref, e_scr, p_scr):
    @pl.when(pl.program_id(1) == 0)
    def _():
        e_scr[...] = _rms(h_ref[...], g_ref[...]).astype(BF16)
        p_scr[...] = p_ref[...].astype(BF16)

    gate = _sigmoid(jnp.dot(e_scr[...], wg_ref[...], preferred_element_type=F32))
    emb = jnp.dot(p_scr[...], wp_ref[...], preferred_element_type=F32)
    o_ref[...] = hcol_ref[...] + gate * emb


def _ple(h, g, p, w_gate, w_proj, layer, *, tm, tn):
    t, d = h.shape
    pd = p.shape[2]
    tm = min(tm, t)
    return pl.pallas_call(
        _ple_kernel,
        grid=(t // tm, d // tn),
        in_specs=[
            pl.BlockSpec((tm, d), lambda i, j: (i, 0)),
            pl.BlockSpec((tm, tn), lambda i, j: (i, j)),
            pl.BlockSpec((1, d), lambda i, j: (0, 0)),
            pl.BlockSpec((None, tm, pd), lambda i, j: (layer, i, 0)),
            pl.BlockSpec((None, d, tn), lambda i, j: (layer, 0, j)),
            pl.BlockSpec((None, pd, tn), lambda i, j: (layer, 0, j)),
        ],
        out_specs=pl.BlockSpec((tm, tn), lambda i, j: (i, j)),
        out_shape=jax.ShapeDtypeStruct((t, d), F32),
        scratch_shapes=[pltpu.VMEM((tm, d), BF16), pltpu.VMEM((tm, pd), BF16)],
        compiler_params=_params("parallel", "arbitrary"),
        name="ple",
    )(h, h, g.reshape(1, d), p, w_gate, w_proj)


def _pad_w_in(w):
    lead = w.shape[:-1]
    cut = CB_BF * HEAD_DIM
    w = w.astype(BF16)
    zeros = functools.partial(jnp.zeros, dtype=BF16)
    return jnp.concatenate(
        [w[..., :cut], w[..., cut:cut + B_HEADS], zeros(lead + (HEAD_DIM - B_HEADS,)), w[..., cut + B_HEADS:],
         zeros(lead + (HEAD_DIM,))], axis=-1)


def _mixers(proj3, lb, hgrn_onorm, fox_bf, fox_qn, fox_kn, moba_qn, moba_kn):
    bsz, s, _ = proj3.shape
    oa = _hgrn(proj3, lb, hgrn_onorm, ts=512)

    fcum = _fox_fcum(proj3, fox_bf, ts=512)
    bq, bk, bv = _fox_prep(proj3, fcum, fox_qn, fox_kn, ts=1024)
    ob = _attn(bq, bk, bv, None, hp=FOX_HEADS_PER_STEP, name="fox_attn")

    cq, ck, cv, sel = _moba_prep(proj3, moba_qn, moba_kn)
    oc = _attn(cq, ck, cv, sel, hp=MOBA_HEADS_PER_STEP, name="moba_attn")
    return oa, ob, oc


def kernel(x, p, attn_norm, w_in, fox_bf, lb_logits, hgrn_onorm, fox_qnorm, fox_knorm, moba_qnorm,
           moba_knorm, w_out, ffn_norm, w_gate, w_up, conv_w, conv_b, w_down, ple_norm, w_ple_gate,
           w_ple_proj):
    bsz, s, d = x.shape
    depth = w_in.shape[0]
    t = bsz * s
    lbs = _lower_bounds(lb_logits)
    h = x.reshape(t, d).astype(F32)
    w_in = _pad_w_in(w_in)
    w_out, w_gate, w_up, w_down, w_ple_gate, w_ple_proj = (
        w.astype(BF16) for w in (w_out, w_gate, w_up, w_down, w_ple_gate, w_ple_proj))
    p = p.reshape(depth, t, -1)
    for i in range(depth):
        proj = _norm_matmul(h, attn_norm[i], w_in, i, tm=1024, tn=768)
        proj3 = proj.reshape(bsz, s, IN_COLS_PADDED)
        oa, ob, oc = _mixers(proj3, lbs[i], hgrn_onorm[i], fox_bf[i], fox_qnorm[i], fox_knorm[i],
                             moba_qnorm[i], moba_knorm[i])
        h = _out_proj(h, oa.reshape(t, A_W), ob.reshape(t, B_W), oc.reshape(t, C_W), w_out, i,
                      tm=1024, tn=1024)
        h = _ffn(h, ffn_norm[i], w_gate, w_up, conv_w[i], conv_b[i], w_down, i, seq=s, tm=512, tf=512)
        h = _ple(h, ple_norm[i], p, w_ple_gate, w_ple_proj, i, tm=1024, tn=1024)
    return h.reshape(bsz, s, d).astype(x.dtype)
```

```python
import functools

import numpy as np
import jax
import jax.numpy as jnp
from jax import lax
from jax.experimental import pallas as pl
from jax.experimental.pallas import tpu as pltpu

F32 = jnp.float32
BF16 = jnp.bfloat16
HIGHEST = lax.Precision.HIGHEST

HEAD_DIM = 128
A_HEADS = 4
B_HEADS = 6
C_HEADS = 6
A_W = A_HEADS * HEAD_DIM
B_W = B_HEADS * HEAD_DIM
C_W = C_HEADS * HEAD_DIM
MOBA_BLOCK = 256
MOBA_TOPK = 3
EPS = 1e-6
TINY = 1e-30
NEG = -1e30
LOG2E = 1.4426950408889634
Q_SCALE = HEAD_DIM ** -0.5 * LOG2E

CB_AQ, CB_AF, CB_AI, CB_AG = 0, 4, 8, 12
CB_BQ, CB_BK, CB_BV, CB_BF = 16, 22, 28, 34
CB_CQ, CB_CK, CB_CV = 35, 41, 47
IN_COLS_PADDED = 54 * HEAD_DIM

HGRN_CHUNK = 128
FOX_HEADS_PER_STEP = 6
MOBA_HEADS_PER_STEP = 6
VMEM_LIMIT = 56 * 1024 * 1024

NT_DIMS = (((1,), (1,)), ((), ()))


def _params(*sem):
    return pltpu.CompilerParams(dimension_semantics=sem, vmem_limit_bytes=VMEM_LIMIT)


def _sigmoid(x):
    return 1.0 / (1.0 + jnp.exp(-x))


def _rms(x, g):
    ms = jnp.mean(x * x, axis=-1, keepdims=True)
    return (x * lax.rsqrt(ms + EPS)) * g


def _lower_bounds_kernel(x_ref, o_ref):
    x = x_ref[...]
    depth = x.shape[0]
    m = x[0:1]
    for i in range(1, depth):
        m = jnp.maximum(m, x[i:i + 1])
    e = jnp.exp(x - m)
    tot = e[0:1]
    for i in range(1, depth):
        tot = tot + e[i:i + 1]
    sm = e / tot
    run = jnp.zeros_like(m)
    for i in range(depth):
        run = run + sm[i:i + 1]
        o_ref[i:i + 1, :] = run - sm[0:1]


def _lower_bounds(lb_logits):
    return pl.pallas_call(
        _lower_bounds_kernel,
        out_shape=jax.ShapeDtypeStruct(lb_logits.shape, F32),
    )(lb_logits.astype(F32))


def _norm_matmul_kernel(x0_ref, xn_ref, g_ref, w_ref, o_ref, a_even, a_odd, *, chunks):
    i = pl.program_id(0)
    j = pl.program_id(1)
    rows = a_even.shape[0] // chunks
    dst = pl.ds(pl.multiple_of(jnp.minimum(j, chunks - 1) * rows, rows), rows)

    @pl.when((i == 0) & (j == 0))
    def _():
        a_even[...] = _rms(x0_ref[...], g_ref[...]).astype(BF16)

    def step(cur, nxt):
        nxt[dst, :] = _rms(xn_ref[...], g_ref[...]).astype(BF16)
        o_ref[...] = jnp.dot(cur[...], w_ref[...], preferred_element_type=F32)

    @pl.when(i % 2 == 0)
    def _():
        step(a_even, a_odd)

    @pl.when(i % 2 == 1)
    def _():
        step(a_odd, a_even)


def _norm_matmul(x, g, w, layer, *, tm, tn):
    t, k = x.shape
    n = w.shape[2]
    tm = min(tm, t)
    nt, nj = t // tm, n // tn
    chunks = min(nj, 8)
    rows = tm // chunks
    scratch = pltpu.VMEM((tm, k), BF16)

    def next_chunk(i, j):
        return (jnp.minimum(i + 1, nt - 1) * chunks + jnp.minimum(j, chunks - 1), 0)

    return pl.pallas_call(
        functools.partial(_norm_matmul_kernel, chunks=chunks),
        grid=(nt, nj),
        in_specs=[
            pl.BlockSpec((tm, k), lambda i, j: (0, 0), pipeline_mode=pl.Buffered(1)),
            pl.BlockSpec((rows, k), next_chunk),
            pl.BlockSpec((1, k), lambda i, j: (0, 0)),
            pl.BlockSpec((None, k, tn), lambda i, j: (layer, 0, j)),
        ],
        out_specs=pl.BlockSpec((tm, tn), lambda i, j: (i, j)),
        out_shape=jax.ShapeDtypeStruct((t, n), F32),
        scratch_shapes=[scratch, scratch],
        compiler_params=_params("arbitrary", "arbitrary"),
        name="norm_in_proj",
    )(x, x, g.reshape(1, k), w)


def _hgrn_levels(chunk):
    levels = []
    m = chunk // 2
    while m >= 1:
        levels.append(m)
        m //= 2
    return levels


def _hgrn_masks(chunk):
    t = np.arange(chunk)[:, None]
    s = np.arange(chunk)[None, :]
    out = []
    for m in _hgrn_levels(chunk):
        same_parent = (t // (2 * m)) == (s // (2 * m))
        out.append((same_parent & ((t // m) % 2 == 1) & ((s // m) % 2 == 0)).astype(np.float32))
    return np.stack(out)


def _hgrn_kernel(q_ref, f_ref, i_ref, g_ref, lb_ref, on_ref, tri_ref, mask_ref, o_ref, st_scr, b_scr,
                 *, chunk, n_chunks):
    @pl.when(pl.program_id(1) == 0)
    def _():
        st_scr[...] = jnp.zeros_like(st_scr)

    width = A_W
    lb = lb_ref[...]
    one_m_lb = 1.0 - lb
    onorm = on_ref[...]
    tri = tri_ref[...]
    row = lax.broadcasted_iota(jnp.int32, (chunk, width), 0)
    levels = _hgrn_levels(chunk)
    heads = [slice(h * HEAD_DIM, (h + 1) * HEAD_DIM) for h in range(A_HEADS)]

    def boundary_rows(m):
        if 2 * m >= 8:
            pieces = [
                jnp.broadcast_to(b_scr[2 * m * p + m - 1:2 * m * p + m, :], (2 * m, width))
                for p in range(chunk // (2 * m))
            ]
            return pieces[0] if len(pieces) == 1 else jnp.concatenate(pieces, axis=0)
        assert m == 2
        lo = jnp.concatenate(
            [jnp.broadcast_to(b_scr[8 * p + 1:8 * p + 2, :], (8, width)) for p in range(chunk // 8)], axis=0)
        hi = jnp.concatenate(
            [jnp.broadcast_to(b_scr[8 * p + 5:8 * p + 6, :], (8, width)) for p in range(chunk // 8)], axis=0)
        return jnp.where((row & 4) == 0, lo, hi)

    def one_chunk(c, carry):
        rows = pl.ds(pl.multiple_of(c * chunk, chunk), chunk)
        z = f_ref[rows, :]
        f = lb + one_m_lb * _sigmoid(z)
        logf = jnp.log2(jnp.maximum(f, TINY))
        k = one_m_lb * _sigmoid(-z)
        qraw = q_ref[rows, :]
        q = qraw * _sigmoid(qraw)
        v = i_ref[rows, :]
        v16 = v.astype(BF16)

        b = jnp.dot(tri, logf, precision=HIGHEST, preferred_element_type=F32)
        b_scr[...] = b

        b_last = b_scr[chunk - 1:chunk, :]
        qe = (q * jnp.exp2(b)).astype(BF16)
        k_dec = (k * jnp.exp2(b_last - b)).astype(BF16)
        st = [st_scr[h] for h in range(A_HEADS)]
        o_inter = [lax.dot_general(qe[:, hs], st[h].astype(BF16), NT_DIMS, preferred_element_type=F32)
                   for h, hs in enumerate(heads)]
        upd = [jnp.dot(v[:, hs].T.astype(BF16), k_dec[:, hs], preferred_element_type=F32) for hs in heads]
        decay_last = jnp.exp2(b_last)
        for h, hs in enumerate(heads):
            st_scr[h] = st[h] * decay_last[:, hs] + upd[h]

        qts, kts = [], []
        for m in levels:
            right = (row & m) != 0
            if m == 1:
                arg = jnp.where(right, logf, 0.0)
            else:
                d = b - boundary_rows(m)
                arg = jnp.where(right, d, -d)
            scaled = jnp.where(right, q, k) * jnp.exp2(arg)
            qts.append(jnp.where(right, scaled, 0.0).astype(BF16))
            kts.append(jnp.where(right, 0.0, scaled).astype(BF16))
        prods = [[lax.dot_general(qts[lev][:, hs], kts[lev][:, hs], NT_DIMS, preferred_element_type=F32)
                  for hs in heads] for lev in range(len(levels))]
        attn = []
        for h in range(A_HEADS):
            a = mask_ref[0] * prods[0][h]
            for lev in range(1, len(levels)):
                a = a + mask_ref[lev] * prods[lev][h]
            attn.append(a.astype(BF16))
        o_intra = [jnp.dot(attn[h], v16[:, hs], preferred_element_type=F32) for h, hs in enumerate(heads)]

        qk = q * k
        gate = g_ref[rows, :]
        gate = gate * _sigmoid(gate)
        for h, hs in enumerate(heads):
            diag = jnp.sum(qk[:, hs], axis=-1, keepdims=True)
            o = o_intra[h] + o_inter[h] + diag * v[:, hs]
            o_ref[rows, hs] = (_rms(o, onorm) * gate[:, hs]).astype(o_ref.dtype)
        return carry

    lax.fori_loop(0, n_chunks, one_chunk, 0)


def _hgrn(proj3, lb, onorm, *, ts):
    bsz, s, _ = proj3.shape
    ts = min(ts, s)
    chunk = HGRN_CHUNK
    tri = jnp.asarray(np.tril(np.ones((chunk, chunk), np.float32)))
    masks = jnp.asarray(_hgrn_masks(chunk))
    nlev = masks.shape[0]

    def group(cb):
        return pl.BlockSpec((None, ts, A_W), lambda b, i: (b, i, cb // A_HEADS))

    return pl.pallas_call(
        functools.partial(_hgrn_kernel, chunk=chunk, n_chunks=ts // chunk),
        grid=(bsz, s // ts),
        in_specs=[
            group(CB_AQ), group(CB_AF), group(CB_AI), group(CB_AG),
            pl.BlockSpec((1, A_W), lambda b, i: (0, 0)),
            pl.BlockSpec((1, HEAD_DIM), lambda b, i: (0, 0)),
            pl.BlockSpec((chunk, chunk), lambda b, i: (0, 0)),
            pl.BlockSpec((nlev, chunk, chunk), lambda b, i: (0, 0, 0)),
        ],
        out_specs=pl.BlockSpec((None, ts, A_W), lambda b, i: (b, i, 0)),
        out_shape=jax.ShapeDtypeStruct((bsz, s, A_W), BF16),
        scratch_shapes=[pltpu.VMEM((A_HEADS, HEAD_DIM, HEAD_DIM), F32), pltpu.VMEM((chunk, A_W), F32)],
        compiler_params=_params("parallel", "arbitrary"),
        name="hgrn2",
    )(proj3, proj3, proj3, proj3, lb.reshape(1, A_W), onorm.reshape(1, HEAD_DIM), tri, masks)


ATT_BLOCK = 256
ATT_K = 2 * HEAD_DIM
V_ROWS = HEAD_DIM + 16
N_BIAS_PIECES = 3


def _bias_columns(bias):
    lane = lax.broadcasted_iota(jnp.int32, (bias.shape[0], HEAD_DIM), 1)
    out = jnp.zeros((bias.shape[0], HEAD_DIM), F32)
    rest = bias
    for i in range(N_BIAS_PIECES):
        piece = rest.astype(BF16).astype(F32)
        out = jnp.where(lane == i, piece, out)
        rest = rest - piece
    return out


def _store_q_blocks(o_ref, q, bias_sign):
    row = lax.broadcasted_iota(jnp.int32, (ATT_K - HEAD_DIM, ATT_BLOCK), 0)
    tail = jnp.where(row < N_BIAS_PIECES, bias_sign, 0.0).astype(F32)
    for j in range(q.shape[0] // ATT_BLOCK):
        qt = q[j * ATT_BLOCK:(j + 1) * ATT_BLOCK].T
        o_ref[j] = jnp.concatenate([qt, tail], axis=0).astype(o_ref.dtype)


def _store_v_blocks(o_ref, v):
    row = lax.broadcasted_iota(jnp.int32, (V_ROWS - HEAD_DIM, ATT_BLOCK), 0)
    tail = jnp.where(row == 0, 1.0, 0.0).astype(F32)
    for j in range(v.shape[0] // ATT_BLOCK):
        vt = v[j * ATT_BLOCK:(j + 1) * ATT_BLOCK].T
        o_ref[j] = jnp.concatenate([vt, tail], axis=0).astype(o_ref.dtype)


def _fox_prep_kernel(q_ref, k_ref, v_ref, fc_ref, qg_ref, kg_ref, qo_ref, ko_ref, vo_ref):
    _store_q_blocks(qo_ref, _rms(q_ref[...], qg_ref[...]) * Q_SCALE, -1.0)
    fc = fc_ref[...]
    lane = lax.broadcasted_iota(jnp.int32, fc.shape, 1)
    mine = jnp.sum(jnp.where(lane == pl.program_id(1), fc, 0.0), axis=-1, keepdims=True)
    kn = _rms(k_ref[...], kg_ref[...])
    ko_ref[...] = jnp.concatenate([kn, _bias_columns(mine)], axis=1).astype(BF16)
    _store_v_blocks(vo_ref, v_ref[...])


def _att_prep_shapes(bsz, nh, s):
    nblk = s // ATT_BLOCK
    return [jax.ShapeDtypeStruct((bsz, nh, nblk, ATT_K, ATT_BLOCK), BF16),
            jax.ShapeDtypeStruct((bsz, nh, s, ATT_K), BF16),
            jax.ShapeDtypeStruct((bsz, nh, nblk, V_ROWS, ATT_BLOCK), BF16)]


def _fox_prep(proj3, fcum, qg, kg, *, ts):
    bsz, s, _ = proj3.shape
    ts = min(ts, s)
    nblk = ts // ATT_BLOCK

    def col(cb):
        return pl.BlockSpec((None, ts, HEAD_DIM), lambda b, h, i: (b, i, cb + h))

    gain = pl.BlockSpec((1, HEAD_DIM), lambda b, h, i: (0, 0))
    return pl.pallas_call(
        _fox_prep_kernel,
        grid=(bsz, B_HEADS, s // ts),
        in_specs=[col(CB_BQ), col(CB_BK), col(CB_BV),
                  pl.BlockSpec((None, ts, HEAD_DIM), lambda b, h, i: (b, i, 0)), gain, gain],
        out_specs=[pl.BlockSpec((None, None, nblk, ATT_K, ATT_BLOCK), lambda b, h, i: (b, h, i, 0, 0)),
                   pl.BlockSpec((None, None, ts, ATT_K), lambda b, h, i: (b, h, i, 0)),
                   pl.BlockSpec((None, None, nblk, V_ROWS, ATT_BLOCK), lambda b, h, i: (b, h, i, 0, 0))],
        out_shape=_att_prep_shapes(bsz, B_HEADS, s),
        compiler_params=_params("parallel", "parallel", "parallel"),
        name="fox_prep",
    )(proj3, proj3, proj3, fcum, qg.reshape(1, HEAD_DIM), kg.reshape(1, HEAD_DIM))


def _fox_fcum_kernel(f_ref, bf_ref, tri_ref, o_ref, carry_scr):
    @pl.when(pl.program_id(1) == 0)
    def _():
        carry_scr[...] = jnp.zeros_like(carry_scr)

    x = f_ref[...] + bf_ref[...]
    logf = jnp.minimum(x, 0.0) - jnp.log(1.0 + jnp.exp(-jnp.abs(x)))
    cum = jnp.dot(tri_ref[...], logf, precision=HIGHEST, preferred_element_type=F32) + carry_scr[...]
    o_ref[...] = cum * LOG2E
    carry_scr[...] = cum[cum.shape[0] - 1:, :]


def _fox_fcum(proj3, b_f, *, ts):
    bsz, s, _ = proj3.shape
    ts = min(ts, s)
    tri = jnp.asarray(np.tril(np.ones((ts, ts), np.float32)))
    bf_pad = jnp.zeros((1, HEAD_DIM), F32).at[0, :B_HEADS].set(b_f.astype(F32))
    return pl.pallas_call(
        _fox_fcum_kernel,
        grid=(bsz, s // ts),
        in_specs=[
            pl.BlockSpec((None, ts, HEAD_DIM), lambda b, i: (b, i, CB_BF)),
            pl.BlockSpec((1, HEAD_DIM), lambda b, i: (0, 0)),
            pl.BlockSpec((ts, ts), lambda b, i: (0, 0)),
        ],
        out_specs=pl.BlockSpec((None, ts, HEAD_DIM), lambda b, i: (b, i, 0)),
        out_shape=jax.ShapeDtypeStruct((bsz, s, HEAD_DIM), F32),
        scratch_shapes=[pltpu.VMEM((1, HEAD_DIM), F32)],
        compiler_params=_params("parallel", "arbitrary"),
        name="fox_fcum",
    )(proj3, bf_pad, tri)


def _flash_update(m, acc_ref, h, s, vt, m_new=None, sub=None):
    if m_new is None:
        m_new = jnp.maximum(m, jnp.max(s, axis=0, keepdims=True))
        sub = m_new
    p = jnp.exp2(s - sub)
    alpha = jnp.exp2(m - m_new)
    acc_ref[h] = alpha * acc_ref[h] + jnp.dot(vt, p.astype(BF16), preferred_element_type=F32)
    return m_new


def _flash_init(acc_ref, hp):
    acc_ref[...] = jnp.zeros_like(acc_ref)
    return tuple(jnp.full((1, ATT_BLOCK), NEG, F32) for _ in range(hp))


def _flash_store(o_ref, acc_ref, h):
    acc = acc_ref[h]
    out = acc[:HEAD_DIM] / acc[HEAD_DIM:HEAD_DIM + 1]
    o_ref[:, h * HEAD_DIM:(h + 1) * HEAD_DIM] = out.T.astype(o_ref.dtype)


def _key_rows(kj):
    return pl.ds(pl.multiple_of(kj * ATT_BLOCK, ATT_BLOCK), ATT_BLOCK)


def _run_blocks(qi, init, put_scores, past_block, own_block, s_even, s_odd):
    def pair(jj, ms):
        kj = 2 * jj
        put_scores(s_odd, kj + 1)
        ms = past_block(kj, ms, s_even)
        put_scores(s_even, kj + 2)
        return past_block(kj + 1, ms, s_odd)

    put_scores(s_even, 0)
    ms = lax.fori_loop(0, qi // 2, pair, init)

    @pl.when(qi % 2 == 0)
    def _():
        own_block(ms, s_even)

    @pl.when(qi % 2 == 1)
    def _():
        put_scores(s_odd, qi)
        own_block(past_block(qi - 1, ms, s_even), s_odd)


def _attn_kernel(q_ref, k_ref, v_ref, *rest, hp, selective):
    if selective:
        sel_ref, o_ref, acc_ref, s_even, s_odd = rest
    else:
        o_ref, acc_ref, s_even, s_odd = rest
    qi = pl.program_id(2)

    def put_scores(s_ref, kj):
        for h in range(hp):
            s_ref[h] = jnp.dot(k_ref[h, _key_rows(kj), :], q_ref[h], preferred_element_type=F32)

    def past_block(kj, ms, s_ref):
        out = []
        for h in range(hp):
            s = s_ref[h]
            if selective:
                chosen = sel_ref[h, pl.ds(kj, 1), :] > 0.5
                blk_max = jnp.max(s, axis=0, keepdims=True)
                m_new = jnp.maximum(ms[h], jnp.where(chosen, blk_max, NEG))
                sub = jnp.where(chosen, m_new, jnp.inf)
                out.append(_flash_update(ms[h], acc_ref, h, s, v_ref[h, kj], m_new=m_new, sub=sub))
            else:
                out.append(_flash_update(ms[h], acc_ref, h, s, v_ref[h, kj]))
        return tuple(out)

    def own_block(ms, s_ref):
        kpos = lax.broadcasted_iota(jnp.int32, (ATT_BLOCK, ATT_BLOCK), 0)
        qpos = lax.broadcasted_iota(jnp.int32, (ATT_BLOCK, ATT_BLOCK), 1)
        for h in range(hp):
            s = jnp.where(kpos <= qpos, s_ref[h], NEG)
            _flash_update(ms[h], acc_ref, h, s, v_ref[h, qi])
            _flash_store(o_ref, acc_ref, h)

    _run_blocks(qi, _flash_init(acc_ref, hp), put_scores, past_block, own_block, s_even, s_odd)


def _attn(qt, kn, vt, sel, *, hp, name):
    bsz, nh, s, _ = kn.shape
    nblk = s // ATT_BLOCK
    once = pl.Buffered(1)
    in_specs = [
        pl.BlockSpec((None, hp, None, ATT_K, ATT_BLOCK), lambda b, h, i: (b, h, i, 0, 0)),
        pl.BlockSpec((None, hp, s, ATT_K), lambda b, h, i: (b, h, 0, 0), pipeline_mode=once),
        pl.BlockSpec((None, hp, nblk, V_ROWS, ATT_BLOCK), lambda b, h, i: (b, h, 0, 0, 0), pipeline_mode=once),
    ]
    args = [qt, kn, vt]
    if sel is not None:
        in_specs.append(pl.BlockSpec((None, hp, sel.shape[2], ATT_BLOCK), lambda b, h, i: (b, h, 0, i)))
        args.append(sel)
    scores = pltpu.VMEM((hp, ATT_BLOCK, ATT_BLOCK), F32)
    return pl.pallas_call(
        functools.partial(_attn_kernel, hp=hp, selective=sel is not None),
        grid=(bsz, nh // hp, s // ATT_BLOCK),
        in_specs=in_specs,
        out_specs=pl.BlockSpec((None, ATT_BLOCK, hp * HEAD_DIM), lambda b, h, i: (b, i, h)),
        out_shape=jax.ShapeDtypeStruct((bsz, s, nh * HEAD_DIM), BF16),
        scratch_shapes=[pltpu.VMEM((hp, V_ROWS, ATT_BLOCK), F32), scores, scores],
        compiler_params=_params("parallel", "parallel", "arbitrary"),
        name=name,
    )(*args)


def _moba_prep_kernel(q_ref, k_ref, v_ref, qg_ref, kg_ref, bias_ref, qo_ref, ko_ref, vo_ref, sel_ref, *, topk):
    qn = _rms(q_ref[...], qg_ref[...])
    kn = _rms(k_ref[...], kg_ref[...])
    s = qn.shape[0]
    _store_q_blocks(qo_ref, qn * Q_SCALE, 1.0)
    ko_ref[...] = jnp.concatenate([kn.astype(BF16), bias_ref[...]], axis=1)
    _store_v_blocks(vo_ref, v_ref[...])

    nb = s // MOBA_BLOCK
    nbp = sel_ref.shape[0]
    kbar = jnp.sum(kn.reshape(nb, MOBA_BLOCK, HEAD_DIM), axis=1) * (1.0 / MOBA_BLOCK)
    if nbp > nb:
        kbar = jnp.concatenate([kbar, jnp.zeros((nbp - nb, HEAD_DIM), F32)], axis=0)
    gate = lax.dot_general(kbar, qn, NT_DIMS, precision=HIGHEST, preferred_element_type=F32)
    blk = lax.broadcasted_iota(jnp.int32, (nbp, s), 0)
    own = lax.broadcasted_iota(jnp.int32, (nbp, s), 1) // MOBA_BLOCK
    gate = jnp.where(blk < own, gate, NEG)
    sel = jnp.zeros((nbp, s), F32)
    for _ in range(topk):
        mx = jnp.max(gate, axis=0, keepdims=True)
        idx = jnp.min(jnp.where(gate == mx, blk, nbp), axis=0, keepdims=True)
        hit = blk == idx
        sel = jnp.where(hit & (mx > NEG / 2), 1.0, sel)
        gate = jnp.where(hit, -jnp.inf, gate)
    sel_ref[...] = sel


def _alibi_key_bias(s):
    slopes = np.exp2(-8.0 * np.arange(1, C_HEADS + 1, dtype=np.float32) / C_HEADS).astype(np.float32)
    slopes = slopes * np.float32(LOG2E)
    rest = slopes[:, None] * np.arange(s, dtype=np.float32)[None, :]
    out = np.zeros((C_HEADS, s, HEAD_DIM), np.float32)
    for i in range(N_BIAS_PIECES):
        piece = rest.astype(BF16).astype(np.float32)
        out[:, :, i] = piece
        rest = rest - piece
    return out.astype(BF16)


def _moba_prep(proj3, qg, kg):
    assert ATT_BLOCK == MOBA_BLOCK
    bsz, s, _ = proj3.shape
    nb = s // MOBA_BLOCK
    nbp = -(-nb // 8) * 8
    topk = min(MOBA_TOPK, nb - 1)
    bias = jnp.asarray(_alibi_key_bias(s))

    def col(cb):
        return pl.BlockSpec((None, s, HEAD_DIM), lambda b, h: (b, 0, cb + h))

    gain = pl.BlockSpec((1, HEAD_DIM), lambda b, h: (0, 0))
    nblk = s // ATT_BLOCK
    return pl.pallas_call(
        functools.partial(_moba_prep_kernel, topk=topk),
        grid=(bsz, C_HEADS),
        in_specs=[col(CB_CQ), col(CB_CK), col(CB_CV), gain, gain,
                  pl.BlockSpec((None, s, HEAD_DIM), lambda b, h: (h, 0, 0))],
        out_specs=[pl.BlockSpec((None, None, nblk, ATT_K, ATT_BLOCK), lambda b, h: (b, h, 0, 0, 0)),
                   pl.BlockSpec((None, None, s, ATT_K), lambda b, h: (b, h, 0, 0)),
                   pl.BlockSpec((None, None, nblk, V_ROWS, ATT_BLOCK), lambda b, h: (b, h, 0, 0, 0)),
                   pl.BlockSpec((None, None, nbp, s), lambda b, h: (b, h, 0, 0))],
        out_shape=_att_prep_shapes(bsz, C_HEADS, s) + [jax.ShapeDtypeStruct((bsz, C_HEADS, nbp, s), F32)],
        compiler_params=_params("parallel", "parallel"),
        name="moba_prep",
    )(proj3, proj3, proj3, qg.reshape(1, HEAD_DIM), kg.reshape(1, HEAD_DIM), bias)


def _out_proj_kernel(h_ref, a_ref, b_ref, c_ref, w_ref, o_ref):
    acc = jnp.dot(a_ref[...], w_ref[:A_W, :], preferred_element_type=F32)
    acc = acc + jnp.dot(b_ref[...], w_ref[A_W:A_W + B_W, :], preferred_element_type=F32)
    acc = acc + jnp.dot(c_ref[...], w_ref[A_W + B_W:, :], preferred_element_type=F32)
    o_ref[...] = h_ref[...] + acc


def _out_proj(h, oa, ob, oc, w_out, layer, *, tm, tn):
    t, d = h.shape
    tm = min(tm, t)

    def act(w):
        return pl.BlockSpec((tm, w), lambda i, j: (i, 0))

    return pl.pallas_call(
        _out_proj_kernel,
        grid=(t // tm, d // tn),
        in_specs=[pl.BlockSpec((tm, tn), lambda i, j: (i, j)), act(A_W), act(B_W), act(C_W),
                  pl.BlockSpec((None, A_W + B_W + C_W, tn), lambda i, j: (layer, 0, j))],
        out_specs=pl.BlockSpec((tm, tn), lambda i, j: (i, j)),
        out_shape=jax.ShapeDtypeStruct((t, d), F32),
        compiler_params=_params("parallel", "arbitrary"),
        name="out_proj",
    )(h, oa, ob, oc, w_out)


HALO = 16


def _ffn_kernel(h_ref, halo_ref, g_ref, wg_ref, wu_ref, cw_ref, cb_ref, wd_ref, o_ref, c_scr, acc_scr,
                *, tiles_per_seq):
    i = pl.program_id(0)
    j = pl.program_id(1)
    tm = h_ref.shape[0]

    @pl.when(j == 0)
    def _():
        g = g_ref[...]
        c_scr[HALO:, :] = _rms(h_ref[...], g).astype(BF16)
        prev = jnp.where(i % tiles_per_seq == 0, 0.0, 1.0) * _rms(halo_ref[...], g)
        c_scr[:HALO, :] = prev.astype(BF16)
        acc_scr[...] = jnp.zeros_like(acc_scr)

    c_all = c_scr[...]
    hg = jnp.dot(c_all, wg_ref[...], preferred_element_type=F32)
    hu = jnp.dot(c_all[HALO:], wu_ref[...], preferred_element_type=F32)
    cw = cw_ref[...]
    conv = (hg[HALO - 2:HALO - 2 + tm] * cw[0:1] + hg[HALO - 1:HALO - 1 + tm] * cw[1:2]
            + hg[HALO:] * cw[2:3] + cb_ref[...])
    gelu = 0.5 * conv * (1.0 + jnp.tanh(0.7978845608028654 * (conv + 0.044715 * conv * conv * conv)))
    acc_scr[...] += jnp.dot((gelu * hu).astype(BF16), wd_ref[...], preferred_element_type=F32)

    @pl.when(j == pl.num_programs(1) - 1)
    def _():
        o_ref[...] = h_ref[...] + acc_scr[...]


def _ffn(h, g, w_gate, w_up, conv_w, conv_b, w_down, layer, *, seq, tm, tf):
    t, d = h.shape
    ff = w_gate.shape[2]
    tm = min(tm, seq)
    cw = jnp.zeros((8, ff), F32).at[:conv_w.shape[0]].set(conv_w.astype(F32))
    return pl.pallas_call(
        functools.partial(_ffn_kernel, tiles_per_seq=seq // tm),
        grid=(t // tm, ff // tf),
        in_specs=[
            pl.BlockSpec((tm, d), lambda i, j: (i, 0)),
            pl.BlockSpec((HALO, d), lambda i, j: (jnp.maximum(i * (tm // HALO) - 1, 0), 0)),
            pl.BlockSpec((1, d), lambda i, j: (0, 0)),
            pl.BlockSpec((None, d, tf), lambda i, j: (layer, 0, j)),
            pl.BlockSpec((None, d, tf), lambda i, j: (layer, 0, j)),
            pl.BlockSpec((8, tf), lambda i, j: (0, j)),
            pl.BlockSpec((1, tf), lambda i, j: (0, j)),
            pl.BlockSpec((None, tf, d), lambda i, j: (layer, j, 0)),
        ],
        out_specs=pl.BlockSpec((tm, d), lambda i, j: (i, 0)),
        out_shape=jax.ShapeDtypeStruct((t, d), F32),
        scratch_shapes=[pltpu.VMEM((HALO + tm, d), BF16), pltpu.VMEM((tm, d), F32)],
        compiler_params=_params("parallel", "arbitrary"),
        name="conv_ffn",
    )(h, h, g.reshape(1, d), w_gate, w_up, cw, conv_b.reshape(1, ff).astype(F32), w_down)


def _ple_kernel(h_ref, hcol_ref, g_ref, p_ref, wg_ref, wp_ref, o_ref, e_scr, p_scr):
    @pl.when(pl.program_id(1) == 0)
    def _():
        e_scr[...] = _rms(h_ref[...], g_ref[...]).astype(BF16)
        p_scr[...] = p_ref[...].astype(BF16)

    gate = _sigmoid(jnp.dot(e_scr[...], wg_ref[...], preferred_element_type=F32))
    emb = jnp.dot(p_scr[...], wp_ref[...], preferred_element_type=F32)
    o_ref[...] = hcol_ref[...] + gate * emb


def _ple(h, g, p, w_gate, w_proj, layer, *, tm, tn):
    t, d = h.shape
    pd = p.shape[2]
    tm = min(tm, t)
    return pl.pallas_call(
        _ple_kernel,
        grid=(t // tm, d // tn),
        in_specs=[
            pl.BlockSpec((tm, d), lambda i, j: (i, 0)),
            pl.BlockSpec((tm, tn), lambda i, j: (i, j)),
            pl.BlockSpec((1, d), lambda i, j: (0, 0)),
            pl.BlockSpec((None, tm, pd), lambda i, j: (layer, i, 0)),
            pl.BlockSpec((None, d, tn), lambda i, j: (layer, 0, j)),
            pl.BlockSpec((None, pd, tn), lambda i, j: (layer, 0, j)),
        ],
        out_specs=pl.BlockSpec((tm, tn), lambda i, j: (i, j)),
        out_shape=jax.ShapeDtypeStruct((t, d), F32),
        scratch_shapes=[pltpu.VMEM((tm, d), BF16), pltpu.VMEM((tm, pd), BF16)],
        compiler_params=_params("parallel", "arbitrary"),
        name="ple",
    )(h, h, g.reshape(1, d), p, w_gate, w_proj)


def _pad_w_in(w):
    lead = w.shape[:-1]
    cut = CB_BF * HEAD_DIM
    w = lax.optimization_barrier(w.astype(BF16))
    zeros = functools.partial(jnp.zeros, dtype=BF16)
    return jnp.concatenate(
        [w[..., :cut], w[..., cut:cut + B_HEADS], zeros(lead + (HEAD_DIM - B_HEADS,)), w[..., cut + B_HEADS:],
         zeros(lead + (HEAD_DIM,))], axis=-1)


def _mixers(proj3, lb, hgrn_onorm, fox_bf, fox_qn, fox_kn, moba_qn, moba_kn):
    bsz, s, _ = proj3.shape
    oa = _hgrn(proj3, lb, hgrn_onorm, ts=512)

    fcum = _fox_fcum(proj3, fox_bf, ts=512)
    bq, bk, bv = _fox_prep(proj3, fcum, fox_qn, fox_kn, ts=1024)
    ob = _attn(bq, bk, bv, None, hp=FOX_HEADS_PER_STEP, name="fox_attn")

    cq, ck, cv, sel = _moba_prep(proj3, moba_qn, moba_kn)
    oc = _attn(cq, ck, cv, sel, hp=MOBA_HEADS_PER_STEP, name="moba_attn")
    return oa, ob, oc


def kernel(x, p, attn_norm, w_in, fox_bf, lb_logits, hgrn_onorm, fox_qnorm, fox_knorm, moba_qnorm,
           moba_knorm, w_out, ffn_norm, w_gate, w_up, conv_w, conv_b, w_down, ple_norm, w_ple_gate,
           w_ple_proj):
    bsz, s, d = x.shape
    depth = w_in.shape[0]
    t = bsz * s
    lbs = _lower_bounds(lb_logits)
    h = x.reshape(t, d).astype(F32)
    w_in = _pad_w_in(w_in)
    w_out, w_gate, w_up, w_down, w_ple_gate, w_ple_proj = (
        w.astype(BF16) for w in (w_out, w_gate, w_up, w_down, w_ple_gate, w_ple_proj))
    p = p.reshape(depth, t, -1)
    for i in range(depth):
        proj = _norm_matmul(h, attn_norm[i], w_in, i, tm=1024, tn=768)
        proj3 = proj.reshape(bsz, s, IN_COLS_PADDED)
        oa, ob, oc = _mixers(proj3, lbs[i], hgrn_onorm[i], fox_bf[i], fox_qnorm[i], fox_knorm[i],
                             moba_qnorm[i], moba_knorm[i])
        h = _out_proj(h, oa.reshape(t, A_W), ob.reshape(t, B_W), oc.reshape(t, C_W), w_out, i,
                      tm=1024, tn=1024)
        h = _ffn(h, ffn_norm[i], w_gate, w_up, conv_w[i], conv_b[i], w_down, i, seq=s, tm=512, tf=512)
        h = _ple(h, ple_norm[i], p, w_ple_gate, w_ple_proj, i, tm=1024, tn=1024)
    return h.reshape(bsz, s, d).astype(x.dtype)
```
